```python
import jax
import jax.numpy as jnp
from jax import lax
import numpy as np

D_MODEL = 1024
BATCH = 2
SEQ = 8192
DEPTH = 4
DEC_BATCH = 128
DEC_SEQ = 4
PAST_LEN = 2048
PAGE_SIZE = 128

EPS = 1e-6
NEG = -1e30
LB_FLOOR = 1e-30
ROPE_THETA = 10000.0
CHUNK = 64
A_HEADS = 4
A_DK = 128
A_DV = 128
A_KW = A_HEADS * A_DK
A_VW = A_HEADS * A_DV
B_HEADS = 4
B_DK = 128
B_DV = 128
B_KW = B_HEADS * B_DK
B_VW = B_HEADS * B_DV
B_CONV = 4
B_CONV_CH = 2 * B_KW + B_VW
C_WINDOWS = (128, 512, 2048)
C_DILATIONS = (1, 4, 16)
C_GROUPS = 3
C_HEADS = 4
C_DH = 128
C_W = C_HEADS * C_DH
C_BLK = 128
N_MEM = 256
X_HEADS = 4
X_DH = 128
X_W = X_HEADS * X_DH
D_FF = 128 * ((8 * D_MODEL // 3 + 127) // 128)
FFN_CONV = 3
N_BRANCH = 3
IN_SIZES = (A_KW, A_KW, A_VW, A_VW, B_CONV_CH, B_VW, B_HEADS, B_HEADS,
            C_GROUPS * C_W, C_GROUPS * C_W, C_GROUPS * C_W, N_BRANCH * D_MODEL)
N_IN = sum(IN_SIZES)

kernel_name = 'hybrid_hgrn2_gdn_dilated_swa_decoder_step'


def _rms(x, g):
    xf = x.astype(jnp.float32)
    y = xf * lax.rsqrt(jnp.mean(xf * xf, axis=-1, keepdims=True) + EPS)
    return (y * g.astype(jnp.float32)).astype(x.dtype)


def _l2n(x):
    return x * lax.rsqrt(jnp.sum(x * x, axis=-1, keepdims=True) + EPS)


def _rope(x, pos):
    half = x.shape[-1] // 2
    inv = ROPE_THETA ** (-jnp.arange(half, dtype=jnp.float32) / half)
    ang = pos.astype(jnp.float32)[:, None] * inv[None, :]
    shp = (ang.shape[0],) + (1,) * (x.ndim - 3) + (half,)
    cos, sin = jnp.cos(ang).reshape(shp), jnp.sin(ang).reshape(shp)
    xf = x.astype(jnp.float32)
    x1, x2 = xf[..., :half], xf[..., half:]
    return jnp.concatenate([x1 * cos - x2 * sin, x2 * cos + x1 * sin], axis=-1).astype(x.dtype)


def _causal_dwconv(x, buf, w):
    k_w, t = w.shape[0], x.shape[1]
    xp = jnp.concatenate([buf.astype(x.dtype), x], axis=1)
    y = xp[:, 0:t] * w[0]
    for j in range(1, k_w):
        y = y + xp[:, j:j + t] * w[j]
    return y, xp[:, t:]


def _chunked_scan(step, s0, xs):
    bsz, t = xs[0].shape[:2]
    c = CHUNK if t % CHUNK == 0 else t
    n = t // c
    xs_c = [jnp.moveaxis(a.reshape((bsz, n, c) + a.shape[2:]), 1, 0) for a in xs]
    s, o = lax.scan(lambda st, xc: step(st, *xc), s0, xs_c)
    o = jnp.moveaxis(o, 0, 1)
    return s, o.reshape((bsz, t) + o.shape[3:])


def _masked_decay(diff, mask):
    return jnp.where(mask, jnp.exp(jnp.where(mask, diff, 0.0)), 0.0)


def _gla_chunk(s, q, k, v, logf):
    c = q.shape[1]
    b = jnp.cumsum(logf, axis=1)
    causal = jnp.tril(jnp.ones((c, c), dtype=bool))[None, :, :, None, None]
    diff = b[:, :, None] - b[:, None, :]
    dec = _masked_decay(diff, causal)
    att = jnp.einsum('bthc,bshc,btshc->bhts', q, k, dec)
    o = jnp.einsum('bthc,bhcv->bthv', q * jnp.exp(b), s) + jnp.einsum('bhts,bshv->bthv', att, v)
    bl = b[:, -1]
    s = s * jnp.exp(bl)[..., None] + jnp.einsum('bshc,bshv->bhcv', k * jnp.exp(bl[:, None] - b), v)
    return s, o


def _hgrn2(pq, pf, pi, pg, lb, onorm, s0):
    bsz, t, _ = pq.shape
    f32 = jnp.float32

    def heads(a, d):
        return a.astype(f32).reshape(bsz, t, A_HEADS, d)

    q = jax.nn.silu(heads(pq, A_DK)) * (A_DK ** -0.5)
    lbh = lb.astype(f32).reshape(A_HEADS, A_DK)
    logf = jnp.logaddexp(jnp.log(jnp.maximum(lbh, LB_FLOOR)),
                         jnp.log1p(-lbh) + jax.nn.log_sigmoid(heads(pf, A_DK)))
    k = -jnp.expm1(logf)
    v = heads(pi, A_DV)
    s, o = _chunked_scan(_gla_chunk, s0.astype(f32), [q, k, v, logf])
    o = _rms(o, onorm) * jax.nn.silu(heads(pg, A_DV))
    return o.reshape(bsz, t, A_VW).astype(pq.dtype), s.astype(s0.dtype)


def _gdn_chunk(s, q, k, v, beta, g):
    c = q.shape[1]
    b = jnp.cumsum(g, axis=1)
    diff = (b[:, :, None, :] - b[:, None, :, :]).transpose(0, 3, 1, 2)
    strict = jnp.tril(jnp.ones((c, c), dtype=bool), -1)
    incl = jnp.tril(jnp.ones((c, c), dtype=bool))
    dec_s = _masked_decay(diff, strict)
    dec_i = _masked_decay(diff, incl)
    kk = jnp.einsum('bthc,bshc->bhts', k, k)
    low = beta.transpose(0, 2, 1)[..., None] * kk * dec_s
    rhs = beta[..., None] * (v - jnp.exp(b)[..., None] * jnp.einsum('bthc,bhcv->bthv', k, s))
    u = lax.linalg.triangular_solve(jnp.eye(c, dtype=low.dtype) + low, rhs.transpose(0, 2, 1, 3),
                                    left_side=True, lower=True, unit_diagonal=True)
    qk = jnp.einsum('bthc,bshc->bhts', q, k) * dec_i
    o = jnp.exp(b)[..., None] * jnp.einsum('bthc,bhcv->bthv', q, s) + jnp.einsum('bhts,bhsv->bthv', qk, u)
    bl = b[:, -1]
    s = s * jnp.exp(bl)[..., None, None] + jnp.einsum(
        'bshc,bhsv->bhcv', k * jnp.exp(bl[:, None] - b)[..., None], u)
    return s, o


def _gated_deltanet(pqkv, pz, pbeta, pa, conv_w, a_log, dt_bias, onorm, s0, conv_buf):
    bsz, t, _ = pqkv.shape
    f32 = jnp.float32
    c, new_buf = _causal_dwconv(pqkv, conv_buf, conv_w)
    c = jax.nn.silu(c.astype(f32))
    q = _l2n(c[..., :B_KW].reshape(bsz, t, B_HEADS, B_DK)) * (B_DK ** -0.5)
    k = _l2n(c[..., B_KW:2 * B_KW].reshape(bsz, t, B_HEADS, B_DK))
    v = c[..., 2 * B_KW:].reshape(bsz, t, B_HEADS, B_DV)
    beta = jax.nn.sigmoid(pbeta.astype(f32))
    g = -jnp.exp(a_log.astype(f32)) * jax.nn.softplus(pa.astype(f32) + dt_bias.astype(f32))
    s, o = _chunked_scan(_gdn_chunk, s0.astype(f32), [q, k, v, beta, g])
    o = _rms(o, onorm) * jax.nn.silu(pz.astype(f32).reshape(bsz, t, B_HEADS, B_DV))
    return o.reshape(bsz, t, B_VW).astype(pqkv.dtype), s.astype(s0.dtype), new_buf.astype(conv_buf.dtype)


def _dilated_prompt(q, k, v, dil, band):
    bsz, s, h, dh = q.shape
    span = dil * C_BLK
    sp = -(-s // span) * span
    u = sp // dil
    nb = u // C_BLK

    def sub(a):
        a = jnp.pad(a, ((0, 0), (0, sp - s), (0, 0), (0, 0)))
        a = a.reshape(bsz, u, dil, h, dh).transpose(0, 2, 1, 3, 4)
        return a.reshape(bsz, dil, nb, C_BLK, h, dh)

    def with_prev(a):
        prev = jnp.pad(a, ((0, 0), (0, 0), (1, 0), (0, 0), (0, 0), (0, 0)))[:, :, :-1]
        return jnp.concatenate([prev, a], axis=3)

    qs = sub(q)
    kb = with_prev(sub(k))
    vb = with_prev(sub(v))
    sc = jnp.einsum('bdnqhc,bdnkhc->bdnhqk', qs, kb, preferred_element_type=jnp.float32) * (dh ** -0.5)
    qi = jnp.arange(C_BLK)[:, None] + C_BLK
    ki = jnp.arange(2 * C_BLK)[None, :]
    rel = qi - ki
    valid = ((rel >= 0) & (rel <= band))[None] & ((jnp.arange(nb)[:, None, None] > 0) | (ki >= C_BLK)[None])
    sc = jnp.where(valid[None, None, :, None], sc, NEG)
    m = jnp.max(sc, axis=-1, keepdims=True)
    p = jnp.exp(sc - m)
    l = jnp.sum(p, axis=-1, keepdims=True)
    o = jnp.einsum('bdnhqk,bdnkhc->bdnqhc', (p / l).astype(v.dtype), vb)
    lse = (m + jnp.log(l))[..., 0]
    o = o.reshape(bsz, dil, u, h, dh).transpose(0, 2, 1, 3, 4).reshape(bsz, sp, h, dh)[:, :s]
    lse = lse.transpose(0, 1, 2, 4, 3).reshape(bsz, dil, u, h).transpose(0, 2, 1, 3).reshape(bsz, sp, h)[:, :s]
    return o, lse


def _dilated_sample(q, k, v, kc, vc, dil, band):
    t, cl = q.shape[1], kc.shape[1]
    kpos = (PAST_LEN + jnp.arange(t))[:, None] - dil * jnp.arange(band + 1)[None, :]
    in_new = kpos >= PAST_LEN
    valid = (kpos >= 0) & (kpos >= PAST_LEN - cl)
    ci = jnp.clip(kpos - (PAST_LEN - cl), 0, cl - 1)
    ni = jnp.clip(kpos - PAST_LEN, 0, t - 1)

    def gather(cache, new):
        return jnp.where(in_new[None, :, :, None, None], jnp.take(new, ni, axis=1).astype(cache.dtype),
                         jnp.take(cache, ci, axis=1))

    kg, vg = gather(kc, k), gather(vc, v)
    sc = jnp.einsum('bthc,btjhc->bthj', q.astype(kg.dtype), kg,
                    preferred_element_type=jnp.float32) * (q.shape[-1] ** -0.5)
    sc = jnp.where(valid[None, :, None, :], sc, NEG)
    m = jnp.max(sc, axis=-1, keepdims=True)
    p = jnp.exp(sc - m)
    l = jnp.sum(p, axis=-1, keepdims=True)
    o = jnp.einsum('bthj,btjhc->bthc', (p / l).astype(vg.dtype), vg)
    return o, (m + jnp.log(l))[..., 0]


def _dilated_mixer(cq, ck, cv, pos, caches):
    bsz, t, _ = cq.shape
    shp = (bsz, t, C_GROUPS, C_HEADS, C_DH)
    q = _rope(cq.reshape(shp), pos)
    k = _rope(ck.reshape(shp), pos)
    v = cv.reshape(shp)
    outs, lses, rows = [], [], []
    for gi in range(C_GROUPS):
        win, dil = C_WINDOWS[gi], C_DILATIONS[gi]
        band = win // dil
        qg, kg, vg = q[:, :, gi], k[:, :, gi], v[:, :, gi]
        if caches is None:
            o, lse = _dilated_prompt(qg, kg, vg, dil, band)
            keep = min(win, t)
            rows += [kg[:, t - keep:], vg[:, t - keep:]]
        else:
            o, lse = _dilated_sample(qg, kg, vg, caches[2 * gi], caches[2 * gi + 1], dil, band)
            rows += [kg, vg]
        outs.append(o.astype(jnp.float32))
        lses.append(lse)
    w = jax.nn.softmax(jnp.stack(lses, axis=0), axis=0)
    o = jnp.sum(w[..., None] * jnp.stack(outs, axis=0), axis=0)
    return o.reshape(bsz, t, C_W).astype(cq.dtype), rows


def _mem_kv(mem, g, wk, wv):
    bsz, n, _ = mem.shape
    hm = _rms(mem, g)
    return (hm @ wk).reshape(bsz, n, X_HEADS, X_DH), (hm @ wv).reshape(bsz, n, X_HEADS, X_DH)


def _cross_attn(h, mk, mv, wq, wo):
    bsz, t, _ = h.shape
    q = (h @ wq).reshape(bsz, t, X_HEADS, X_DH)
    sc = jnp.einsum('bthc,bmhc->bhtm', q.astype(mk.dtype), mk, preferred_element_type=jnp.float32) * (X_DH ** -0.5)
    p = jax.nn.softmax(sc, axis=-1).astype(mv.dtype)
    o = jnp.einsum('bhtm,bmhc->bthc', p, mv).reshape(bsz, t, X_W)
    return o.astype(h.dtype) @ wo


def _conv_ffn(h, buf, w_up, cw, cb, w_down):
    u, new_buf = _causal_dwconv(h @ w_up, buf, cw)
    u = u + cb
    y = jax.nn.gelu(u[..., :D_FF], approximate=True) * u[..., D_FF:]
    return y @ w_down, new_buf.astype(buf.dtype)


def _layer(x, mk, mv, pos, P, sa, sb, sbc, c_caches, sf):
    bsz, t, _ = x.shape
    split_at = [int(i) for i in np.cumsum(IN_SIZES)[:-1]]
    h = _rms(x, P['g_mix_pre'])
    (aq, af, ai, ag, bqkv, bz, bbeta, ba, cq, ck, cv, gates) = jnp.split(h @ P['w_in'], split_at, axis=-1)
    ya, sa = _hgrn2(aq, af, ai, ag, P['a_lb'], P['a_onorm'], sa)
    yb, sb, sbc = _gated_deltanet(bqkv, bz, bbeta, ba, P['b_conv_w'], P['b_a_log'], P['b_dt_bias'],
                                  P['b_onorm'], sb, sbc)
    yc, c_rows = _dilated_mixer(cq, ck, cv, pos, c_caches)
    gt = jax.nn.sigmoid(gates.astype(jnp.float32)).reshape(bsz, t, N_BRANCH, D_MODEL)
    merged = (gt[:, :, 0] * (ya @ P['w_pa']) + gt[:, :, 1] * (yb @ P['w_pb'])
              + gt[:, :, 2] * (yc @ P['w_pc']))
    x = x + _rms(merged.astype(x.dtype) @ P['w_o'], P['g_mix_post'])
    h = _rms(x, P['g_x_pre'])
    x = x + _rms(_cross_attn(h, mk, mv, P['w_xq'], P['w_xo']), P['g_x_post'])
    h = _rms(x, P['g_ffn_pre'])
    f, sf = _conv_ffn(h, sf, P['w_up'], P['ffn_conv_w'], P['ffn_conv_b'], P['w_down'])
    x = x + _rms(f, P['g_ffn_post'])
    return x, sa, sb, sbc, c_rows, sf


def setup_inputs(seed: int = 0) -> dict:
    key = jax.random.key(seed)
    keys = iter(jax.random.split(key, 64))

    def nrm(shape, scale=1.0):
        return jax.random.normal(next(keys), shape, jnp.float32) * scale

    def gain(width=D_MODEL):
        return 1.0 + nrm((DEPTH, width), 0.02)

    lw = [min(w, PAST_LEN) for w in C_WINDOWS]
    dt_init = jnp.exp(jax.random.uniform(next(keys), (DEPTH, B_HEADS), jnp.float32,
                                         float(np.log(1e-3)), float(np.log(1e-1))))
    a_init = jax.random.uniform(next(keys), (DEPTH, B_HEADS), jnp.float32, 1.0, 16.0)
    return {
        'x_prompt': nrm((BATCH, SEQ, D_MODEL)),
        'x_sample': nrm((DEC_BATCH, DEC_SEQ, D_MODEL)),
        'mem_prompt': nrm((BATCH, N_MEM, D_MODEL)),
        'state_a': nrm((DEPTH, DEC_BATCH, A_HEADS, A_DK, A_DV), 0.5),
        'state_b': nrm((DEPTH, DEC_BATCH, B_HEADS, B_DK, B_DV), B_DK ** -0.5),
        'state_b_conv': nrm((DEPTH, DEC_BATCH, B_CONV - 1, B_CONV_CH)),
        'cache_c0_k': nrm((DEPTH, DEC_BATCH, lw[0], C_HEADS, C_DH)),
        'cache_c0_v': nrm((DEPTH, DEC_BATCH, lw[0], C_HEADS, C_DH)),
        'cache_c1_k': nrm((DEPTH, DEC_BATCH, lw[1], C_HEADS, C_DH)),
        'cache_c1_v': nrm((DEPTH, DEC_BATCH, lw[1], C_HEADS, C_DH)),
        'cache_c2_k': nrm((DEPTH, DEC_BATCH, lw[2], C_HEADS, C_DH)),
        'cache_c2_v': nrm((DEPTH, DEC_BATCH, lw[2], C_HEADS, C_DH)),
        'state_ffn_conv': nrm((DEPTH, DEC_BATCH, FFN_CONV - 1, 2 * D_FF)),
        'cache_mem_k': nrm((DEPTH, DEC_BATCH, N_MEM, X_HEADS, X_DH)),
        'cache_mem_v': nrm((DEPTH, DEC_BATCH, N_MEM, X_HEADS, X_DH)),
        'g_mix_pre': gain(),
        'g_mix_post': gain(),
        'g_x_pre': gain(),
        'g_x_post': gain(),
        'g_mem': gain(),
        'g_ffn_pre': gain(),
        'g_ffn_post': gain(),
        'w_in': nrm((DEPTH, D_MODEL, N_IN), D_MODEL ** -0.5),
        'a_lb': nrm((DEPTH, A_KW), 0.1),
        'a_onorm': gain(A_DV),
        'b_conv_w': nrm((DEPTH, B_CONV, B_CONV_CH), B_CONV ** -0.5),
        'b_a_log': jnp.log(a_init),
        'b_dt_bias': dt_init + jnp.log(-jnp.expm1(-dt_init)),
        'b_onorm': gain(B_DV),
        'w_pa': nrm((DEPTH, A_VW, D_MODEL), A_VW ** -0.5),
        'w_pb': nrm((DEPTH, B_VW, D_MODEL), B_VW ** -0.5),
        'w_pc': nrm((DEPTH, C_W, D_MODEL), C_W ** -0.5),
        'w_o': nrm((DEPTH, D_MODEL, D_MODEL), D_MODEL ** -0.5),
        'w_xq': nrm((DEPTH, D_MODEL, X_W), D_MODEL ** -0.5),
        'w_xk': nrm((DEPTH, D_MODEL, X_W), D_MODEL ** -0.5),
        'w_xv': nrm((DEPTH, D_MODEL, X_W), D_MODEL ** -0.5),
        'w_xo': nrm((DEPTH, X_W, D_MODEL), X_W ** -0.5),
        'w_up': nrm((DEPTH, D_MODEL, 2 * D_FF), D_MODEL ** -0.5),
        'ffn_conv_w': nrm((DEPTH, FFN_CONV, 2 * D_FF), FFN_CONV ** -0.5),
        'ffn_conv_b': nrm((DEPTH, 2 * D_FF), 0.01),
        'w_down': nrm((DEPTH, D_FF, D_MODEL), D_FF ** -0.5),
    }


def reference(x_prompt, x_sample, mem_prompt, state_a, state_b, state_b_conv,
              cache_c0_k, cache_c0_v, cache_c1_k, cache_c1_v, cache_c2_k, cache_c2_v,
              state_ffn_conv, cache_mem_k, cache_mem_v,
              g_mix_pre, g_mix_post, g_x_pre, g_x_post, g_mem, g_ffn_pre, g_ffn_post,
              w_in, a_lb, a_onorm, b_conv_w, b_a_log, b_dt_bias, b_onorm,
              w_pa, w_pb, w_pc, w_o, w_xq, w_xk, w_xv, w_xo,
              w_up, ffn_conv_w, ffn_conv_b, w_down):
    bp, tp, _ = x_prompt.shape
    ts = x_sample.shape[1]
    dt = x_prompt.dtype
    pos_p = jnp.arange(tp, dtype=jnp.int32)
    pos_s = PAST_LEN + jnp.arange(ts, dtype=jnp.int32)
    sm = jax.nn.softmax(a_lb.astype(jnp.float32), axis=0)
    lb_all = jnp.cumsum(sm, axis=0) - sm[0]
    caches_c = (cache_c0_k, cache_c0_v, cache_c1_k, cache_c1_v, cache_c2_k, cache_c2_v)
    xp, xs = x_prompt, x_sample
    po = [[] for _ in range(12)]
    so = [[] for _ in range(10)]
    for l in range(DEPTH):
        P = dict(g_mix_pre=g_mix_pre[l], g_mix_post=g_mix_post[l], g_x_pre=g_x_pre[l], g_x_post=g_x_post[l],
                 g_ffn_pre=g_ffn_pre[l], g_ffn_post=g_ffn_post[l], w_in=w_in[l], a_lb=lb_all[l],
                 a_onorm=a_onorm[l], b_conv_w=b_conv_w[l], b_a_log=b_a_log[l], b_dt_bias=b_dt_bias[l],
                 b_onorm=b_onorm[l], w_pa=w_pa[l], w_pb=w_pb[l], w_pc=w_pc[l], w_o=w_o[l],
                 w_xq=w_xq[l], w_xo=w_xo[l], w_up=w_up[l], ffn_conv_w=ffn_conv_w[l],
                 ffn_conv_b=ffn_conv_b[l], w_down=w_down[l])
        mk, mv = _mem_kv(mem_prompt, g_mem[l], w_xk[l], w_xv[l])
        xp, sa, sb, sbc, rows, sf = _layer(
            xp, mk, mv, pos_p, P,
            jnp.zeros((bp, A_HEADS, A_DK, A_DV), dt), jnp.zeros((bp, B_HEADS, B_DK, B_DV), dt),
            jnp.zeros((bp, B_CONV - 1, B_CONV_CH), dt), None, jnp.zeros((bp, FFN_CONV - 1, 2 * D_FF), dt))
        for i, a in enumerate([sa, sb, sbc] + rows + [sf, mk, mv]):
            po[i].append(a)
        xs, sa, sb, sbc, rows, sf = _layer(
            xs, cache_mem_k[l], cache_mem_v[l], pos_s, P, state_a[l], state_b[l], state_b_conv[l],
            [c[l] for c in caches_c], state_ffn_conv[l])
        for i, a in enumerate([sa, sb, sbc] + rows + [sf]):
            so[i].append(a)
    (p_state_a, p_state_b, p_state_b_conv, p_c0_k, p_c0_v, p_c1_k, p_c1_v, p_c2_k, p_c2_v,
     p_state_ffn_conv, p_mem_k, p_mem_v) = [jnp.stack(a, axis=0) for a in po]
    (s_state_a, s_state_b, s_state_b_conv, s_c0_k, s_c0_v, s_c1_k, s_c1_v, s_c2_k, s_c2_v,
     s_state_ffn_conv) = [jnp.stack(a, axis=0) for a in so]
    return (xp, xs, p_state_a, p_state_b, p_state_b_conv, p_c0_k, p_c0_v, p_c1_k, p_c1_v, p_c2_k, p_c2_v,
            p_state_ffn_conv, p_mem_k, p_mem_v,
            s_state_a, s_state_b, s_state_b_conv, s_c0_k, s_c0_v, s_c1_k, s_c1_v, s_c2_k, s_c2_v,
            s_state_ffn_conv)
```

```python
import functools
import math

import jax
import jax.numpy as jnp
from jax import lax
from jax.experimental import pallas as pl
from jax.experimental.pallas import tpu as pltpu

F32 = jnp.float32
BF16 = jnp.bfloat16

D_MODEL = 1024
PAST_LEN = 2048
EPS = 1e-6
NEG = -1e30
LB_FLOOR = 1e-30
ROPE_THETA = 10000.0
HEADS = 4
DH = 128
HW = HEADS * DH
B_CONV = 4
B_CONV_CH = 3 * HW
C_WINDOWS = (128, 512, 2048)
C_DILATIONS = (1, 4, 16)
C_GROUPS = 3
C_BAND = 128
C_BLK = 128
N_MEM = 256
D_FF = 128 * ((8 * D_MODEL // 3 + 127) // 128)
FFN_CONV = 3
FFN_COLS = 256
SAMPLE_T_PAD = 8

COL_BQKV = 0
COL_BZ = 1536
COL_A = 2048
COL_CQ = 4096
COL_CK = 5632
COL_CV = 7168
COL_BBA = 8704
COL_GATES = 9216
N_PROJ = 12288

VMEM_LIMIT_BYTES = 56 * 1024 * 1024


def _cparams(*sem):
    return pltpu.CompilerParams(dimension_semantics=sem, vmem_limit_bytes=VMEM_LIMIT_BYTES)


def _dot(a, b):
    return jnp.dot(a.astype(BF16), b.astype(BF16), preferred_element_type=F32)


def _dot_nt(a, b):
    return lax.dot_general(a.astype(BF16), b.astype(BF16), (((1,), (1,)), ((), ())), preferred_element_type=F32)


def _dot_tn(a, b):
    return lax.dot_general(a.astype(BF16), b.astype(BF16), (((0,), (0,)), ((), ())), preferred_element_type=F32)


def _cumsum_rows(tri, x):
    return jnp.dot(tri, x, precision=lax.Precision.HIGHEST, preferred_element_type=F32)


def _rms_rows(x, g):
    return x * lax.rsqrt(jnp.mean(x * x, axis=-1, keepdims=True) + EPS) * g


def _sigmoid(x):
    return jax.nn.sigmoid(x)


def _silu(x):
    return x * jax.nn.sigmoid(x)


def _log1p_exp_neg_abs(x):
    return jnp.log1p(jnp.exp(-jnp.abs(x)))


def _tri(c):
    r = lax.broadcasted_iota(jnp.int32, (c, c), 0)
    s = lax.broadcasted_iota(jnp.int32, (c, c), 1)
    return r, s


def _rms_matmul_kernel(x_ref, g_ref, w_ref, o_ref, xn_ref):
    @pl.when(pl.program_id(1) == 0)
    def _():
        xn_ref[...] = _rms_rows(x_ref[...], g_ref[...]).astype(BF16)

    o_ref[...] = jnp.dot(xn_ref[...], w_ref[...], preferred_element_type=F32)


def rms_matmul(x, g, w, tm, tn):
    m, k = x.shape
    n = w.shape[1]
    return pl.pallas_call(
        _rms_matmul_kernel,
        grid=(m // tm, n // tn),
        in_specs=[pl.BlockSpec((tm, k), lambda i, j: (i, 0)),
                  pl.BlockSpec((1, k), lambda i, j: (0, 0)),
                  pl.BlockSpec((k, tn), lambda i, j: (0, j))],
        out_specs=pl.BlockSpec((tm, tn), lambda i, j: (i, j)),
        out_shape=jax.ShapeDtypeStruct((m, n), F32),
        scratch_shapes=[pltpu.VMEM((tm, k), BF16)],
        compiler_params=_cparams("parallel", "arbitrary"),
        name="rms_matmul",
    )(x, g.reshape(1, k), w)


def _hgrn2_kernel(pq_ref, pf_ref, pi_ref, pg_ref, lb_ref, on_ref, s0_ref, y_ref, s_ref, *, c, tb, t_real):
    j = pl.program_id(1)
    n_j = pl.num_programs(1)

    @pl.when(j == 0)
    def _():
        for h in range(HEADS):
            s_ref[0, h] = s0_ref[0, h].T

    lb = lb_ref[...]
    log_lb = jnp.log(jnp.maximum(lb, LB_FLOOR))
    log_1mlb = jnp.log1p(-lb)
    onorm = on_ref[...]
    r_i, s_i = _tri(c)
    tri = (r_i >= s_i).astype(F32)
    row = lax.broadcasted_iota(jnp.int32, (c, 1), 0)
    n_diag = min(c, t_real)

    def chunk(i, carry):
        r0 = pl.multiple_of(i * c, c)
        rows = pl.ds(r0, c)
        zq = pq_ref[0, rows, :]
        zf = pf_ref[0, rows, :]
        v = pi_ref[0, rows, :]
        zg = pg_ref[0, rows, :]
        q = _silu(zq) * (DH ** -0.5)
        log_sig = jnp.minimum(zf, 0.0) - _log1p_exp_neg_abs(zf)
        t1 = log_1mlb + log_sig
        logf = jnp.maximum(log_lb, t1) + _log1p_exp_neg_abs(log_lb - t1)
        k = -jnp.tanh(0.5 * logf) * (jnp.exp(logf) + 1.0)
        if t_real < tb:
            live = (j * tb + r0 + row) < t_real
            logf = jnp.where(live, logf, 0.0)
            k = jnp.where(live, k, 0.0)
        b = _cumsum_rows(tri, logf)
        bl = b[c - 1:c, :]
        qs = q * jnp.exp(b)
        kd = k * jnp.exp(bl - b)
        el = jnp.exp(bl)
        for h in range(HEADS):
            hs = slice(h * DH, (h + 1) * DH)
            st = s_ref[0, h]
            o = _dot_nt(qs[:, hs], st)
            bh, qh, kh, vh = b[:, hs], q[:, hs], k[:, hs], v[:, hs]
            for s in range(n_diag):
                m = row >= s
                d = jnp.where(m, bh - bh[s:s + 1, :], 0.0)
                a = qh * kh[s:s + 1, :] * jnp.exp(d)
                col = jnp.where(m, jnp.sum(a, axis=-1, keepdims=True), 0.0)
                o = o + col * vh[s:s + 1, :]
            s_ref[0, h] = st * el[:, hs] + _dot_tn(vh, kd[:, hs])
            y_ref[0, rows, hs] = _rms_rows(o, onorm) * _silu(zg[:, hs])
        return carry

    lax.fori_loop(0, tb // c, chunk, 0)

    @pl.when(j == n_j - 1)
    def _():
        for h in range(HEADS):
            s_ref[0, h] = s_ref[0, h].T


def hgrn2(proj3, lb, onorm, s0, c, tb, t_real):
    bsz, t, _ = proj3.shape
    cb = COL_A // HW
    in_specs = [pl.BlockSpec((1, tb, HW), functools.partial(lambda b, j, k: (b, j, k), k=cb + i)) for i in range(4)]
    in_specs += [pl.BlockSpec((1, HW), lambda b, j: (0, 0)),
                 pl.BlockSpec((1, DH), lambda b, j: (0, 0)),
                 pl.BlockSpec((1, HEADS, DH, DH), lambda b, j: (b, 0, 0, 0))]
    return pl.pallas_call(
        functools.partial(_hgrn2_kernel, c=c, tb=tb, t_real=t_real),
        grid=(bsz, t // tb),
        in_specs=in_specs,
        out_specs=[pl.BlockSpec((1, tb, HW), lambda b, j: (b, j, 0)),
                   pl.BlockSpec((1, HEADS, DH, DH), lambda b, j: (b, 0, 0, 0))],
        out_shape=[jax.ShapeDtypeStruct((bsz, t, HW), F32),
                   jax.ShapeDtypeStruct((bsz, HEADS, DH, DH), F32)],
        compiler_params=_cparams("parallel", "arbitrary"),
        name="hgrn2",
    )(proj3, proj3, proj3, proj3, lb.reshape(1, HW), onorm.reshape(1, DH), s0)


def _gdn_kernel(x_ref, z_ref, ba_ref, cw_ref, buf_ref, alog_ref, dt_ref, on_ref, s0_ref,
                y_ref, s_ref, nbuf_ref, win_ref, act_ref, *, c, tb, t_real):
    j = pl.program_id(1)
    n_j = pl.num_programs(1)
    kw = B_CONV - 1

    @pl.when(j == 0)
    def _():
        win_ref[8 - kw:8, :] = buf_ref[0]
        s_ref[...] = s0_ref[...]

    win_ref[8:8 + tb, :] = x_ref[0]
    conv = win_ref[8 - kw:8 - kw + tb, :] * cw_ref[0:1, :]
    for i in range(1, B_CONV):
        conv = conv + win_ref[8 - kw + i:8 - kw + i + tb, :] * cw_ref[i:i + 1, :]
    act_ref[...] = _silu(conv)

    @pl.when(j == n_j - 1)
    def _():
        last = t_real - (t_real - 1) // tb * tb
        nbuf_ref[0] = win_ref[8 + last - kw:8 + last, :]

    win_ref[8 - kw:8, :] = win_ref[8 + tb - kw:8 + tb, :]

    onorm = on_ref[...]
    r_i, s_i = _tri(c)
    tri = (r_i >= s_i).astype(F32)
    strict = r_i > s_i
    incl = r_i >= s_i
    row = lax.broadcasted_iota(jnp.int32, (c, 1), 0)

    def chunk(i, carry):
        r0 = pl.multiple_of(i * c, c)
        rows = pl.ds(r0, c)
        blk = ba_ref[0, rows, :]
        beta_all = _sigmoid(blk)
        sp_in = blk + dt_ref[...]
        softplus = jnp.maximum(sp_in, 0.0) + _log1p_exp_neg_abs(sp_in)
        g_all = -jnp.exp(alog_ref[...]) * softplus
        if t_real < tb:
            live = (j * tb + r0 + row) < t_real
            beta_all = jnp.where(live, beta_all, 0.0)
            g_all = jnp.where(live, g_all, 0.0)
        bg = _cumsum_rows(tri, g_all)
        bg_t = bg.T
        z = z_ref[0, rows, :]
        for h in range(HEADS):
            hs = slice(h * DH, (h + 1) * DH)
            qa = act_ref[rows, h * DH:(h + 1) * DH]
            ka = act_ref[rows, HW + h * DH:HW + (h + 1) * DH]
            vh = act_ref[rows, 2 * HW + h * DH:2 * HW + (h + 1) * DH]
            qh = qa * lax.rsqrt(jnp.sum(qa * qa, axis=-1, keepdims=True) + EPS) * (DH ** -0.5)
            kh = ka * lax.rsqrt(jnp.sum(ka * ka, axis=-1, keepdims=True) + EPS)
            beta = beta_all[:, h:h + 1]
            b_col = bg[:, HEADS + h:HEADS + h + 1]
            b_row = bg_t[HEADS + h:HEADS + h + 1, :]
            diff = b_col - b_row
            dec_s = jnp.where(strict, jnp.exp(jnp.where(strict, diff, 0.0)), 0.0)
            dec_i = jnp.where(incl, jnp.exp(jnp.where(incl, diff, 0.0)), 0.0)
            st = s_ref[0, h]
            k_s = _dot(kh, st)
            q_s = _dot(qh, st)
            low = beta * _dot_nt(kh, kh) * dec_s
            eb = jnp.exp(b_col)
            u = beta * (vh - eb * k_s)
            for t in range(c - 1):
                u = u - low[:, t:t + 1] * u[t:t + 1, :]
            qk = _dot_nt(qh, kh) * dec_i
            o = eb * q_s + _dot(qk, u)
            bl = b_col[c - 1:c, :]
            s_ref[0, h] = st * jnp.exp(bl) + _dot_tn(kh * jnp.exp(bl - b_col), u)
            y_ref[0, rows, hs] = _rms_rows(o, onorm) * _silu(z[:, hs])
        return carry

    lax.fori_loop(0, tb // c, chunk, 0)


def gdn(proj3, conv_w, conv_buf, alog_row, dt_row, onorm, s0, c, tb, t_real):
    bsz, t, _ = proj3.shape
    assert t_real - (t_real - 1) // tb * tb >= B_CONV - 1
    return pl.pallas_call(
        functools.partial(_gdn_kernel, c=c, tb=tb, t_real=t_real),
        grid=(bsz, t // tb),
        in_specs=[pl.BlockSpec((1, tb, B_CONV_CH), lambda b, j: (b, j, COL_BQKV // B_CONV_CH)),
                  pl.BlockSpec((1, tb, HW), lambda b, j: (b, j, COL_BZ // HW)),
                  pl.BlockSpec((1, tb, 128), lambda b, j: (b, j, COL_BBA // 128)),
                  pl.BlockSpec((B_CONV, B_CONV_CH), lambda b, j: (0, 0)),
                  pl.BlockSpec((1, B_CONV - 1, B_CONV_CH), lambda b, j: (b, 0, 0)),
                  pl.BlockSpec((1, 128), lambda b, j: (0, 0)),
                  pl.BlockSpec((1, 128), lambda b, j: (0, 0)),
                  pl.BlockSpec((1, DH), lambda b, j: (0, 0)),
                  pl.BlockSpec((1, HEADS, DH, DH), lambda b, j: (b, 0, 0, 0))],
        out_specs=[pl.BlockSpec((1, tb, HW), lambda b, j: (b, j, 0)),
                   pl.BlockSpec((1, HEADS, DH, DH), lambda b, j: (b, 0, 0, 0)),
                   pl.BlockSpec((1, B_CONV - 1, B_CONV_CH), lambda b, j: (b, 0, 0))],
        out_shape=[jax.ShapeDtypeStruct((bsz, t, HW), F32),
                   jax.ShapeDtypeStruct((bsz, HEADS, DH, DH), F32),
                   jax.ShapeDtypeStruct((bsz, B_CONV - 1, B_CONV_CH), F32)],
        scratch_shapes=[pltpu.VMEM((tb + 8, B_CONV_CH), F32), pltpu.VMEM((tb, B_CONV_CH), F32)],
        compiler_params=_cparams("parallel", "arbitrary"),
        name="gdn",
    )(proj3, proj3, proj3, conv_w, conv_buf, alog_row, dt_row, onorm.reshape(1, DH), s0)


def _rope_kernel(x_ref, cos_ref, sin_ref, o_ref):
    cs = cos_ref[...]
    sn = sin_ref[...]
    for h in range(HEADS):
        hs = slice(h * DH, (h + 1) * DH)
        x = x_ref[:, hs]
        o_ref[:, hs] = x * cs + pltpu.roll(x, DH // 2, 1) * sn


def rope_qk(proj, cos_t, sin_t, tm):
    m = proj.shape[0]
    cb = COL_CQ // HW
    return pl.pallas_call(
        _rope_kernel,
        grid=(m // tm, 2 * C_GROUPS),
        in_specs=[pl.BlockSpec((tm, HW), lambda i, j: (i, cb + j)),
                  pl.BlockSpec((tm, DH), lambda i, j: (i, 0)),
                  pl.BlockSpec((tm, DH), lambda i, j: (i, 0))],
        out_specs=pl.BlockSpec((tm, HW), lambda i, j: (i, j)),
        out_shape=jax.ShapeDtypeStruct((m, 2 * C_GROUPS * HW), F32),
        compiler_params=_cparams("parallel", "arbitrary"),
        name="rope_qk",
    )(proj, cos_t, sin_t)


def _dil_prompt_kernel(q_ref, kp_ref, kc_ref, vp_ref, vc_ref, o_ref, lse_ref):
    n = pl.program_id(2)
    qi, ki = _tri(C_BLK)
    mask_prev = (ki >= qi) & (n > 0)
    mask_cur = ki <= qi
    lane = lax.broadcasted_iota(jnp.int32, (C_BLK, 128), 1)
    lse_blk = jnp.zeros((C_BLK, 128), F32)
    for h in range(HEADS):
        hs = slice(h * DH, (h + 1) * DH)
        q = q_ref[0, :, hs]
        sp = jnp.where(mask_prev, _dot_nt(q, kp_ref[0, :, hs]) * (DH ** -0.5), NEG)
        sc = jnp.where(mask_cur, _dot_nt(q, kc_ref[0, :, hs]) * (DH ** -0.5), NEG)
        m = jnp.maximum(jnp.max(sp, axis=-1, keepdims=True), jnp.max(sc, axis=-1, keepdims=True))
        pp = jnp.exp(sp - m)
        pc = jnp.exp(sc - m)
        l = jnp.sum(pp, axis=-1, keepdims=True) + jnp.sum(pc, axis=-1, keepdims=True)
        o_ref[0, :, hs] = (_dot(pp, vp_ref[0, :, hs]) + _dot(pc, vc_ref[0, :, hs])) / l
        lse_blk = jnp.where(lane == h, m + jnp.log(l), lse_blk)
    lse_ref[0] = lse_blk


def dil_prompt(qk3, proj3, gi):
    bsz, t, _ = qk3.shape
    d = C_DILATIONS[gi]
    u = t // d
    nb = u // C_BLK
    qk_v = qk3.reshape(bsz, u, d * 2 * C_GROUPS * HW)
    pj_v = proj3.reshape(bsz, u, d * N_PROJ)
    nq = 2 * C_GROUPS
    npj = N_PROJ // HW
    cv = COL_CV // HW + gi

    def prev(n):
        return jnp.maximum(n - 1, 0)

    o, lse = pl.pallas_call(
        _dil_prompt_kernel,
        grid=(bsz, d, nb),
        in_specs=[pl.BlockSpec((1, C_BLK, HW), lambda b, r, n: (b, n, r * nq + gi)),
                  pl.BlockSpec((1, C_BLK, HW), lambda b, r, n: (b, prev(n), r * nq + C_GROUPS + gi)),
                  pl.BlockSpec((1, C_BLK, HW), lambda b, r, n: (b, n, r * nq + C_GROUPS + gi)),
                  pl.BlockSpec((1, C_BLK, HW), lambda b, r, n: (b, prev(n), r * npj + cv)),
                  pl.BlockSpec((1, C_BLK, HW), lambda b, r, n: (b, n, r * npj + cv))],
        out_specs=[pl.BlockSpec((1, C_BLK, HW), lambda b, r, n: (b, n, r)),
                   pl.BlockSpec((1, C_BLK, 128), lambda b, r, n: (b, n, r))],
        out_shape=[jax.ShapeDtypeStruct((bsz, u, d * HW), F32),
                   jax.ShapeDtypeStruct((bsz, u, d * 128), F32)],
        compiler_params=_cparams("parallel", "parallel", "arbitrary"),
        name="dil_prompt_g%d" % gi,
    )(qk_v, qk_v, qk_v, pj_v, pj_v)
    return o.reshape(bsz * t, HW), lse.reshape(bsz * t, 128)


def _dil_sample_kernel(qk_ref, v0_ref, v1_ref, v2_ref, k0c, v0c, k1c, v1c, k2c, v2c,
                       o0_ref, o1_ref, o2_ref, l0_ref, l1_ref, l2_ref, *, t_real):
    v_new = (v0_ref, v1_ref, v2_ref)
    kc = (k0c, k1c, k2c)
    vc = (v0c, v1c, v2c)
    o_refs = (o0_ref, o1_ref, o2_ref)
    l_refs = (l0_ref, l1_ref, l2_ref)
    tp = SAMPLE_T_PAD
    row_c = lax.broadcasted_iota(jnp.int32, (C_BAND, 1), 0)
    row_n = lax.broadcasted_iota(jnp.int32, (tp, 1), 0)
    lane = lax.broadcasted_iota(jnp.int32, (1, 128), 1)
    for gi in range(C_GROUPS):
        o_refs[gi][...] = jnp.zeros_like(o_refs[gi])
        l_refs[gi][...] = jnp.zeros_like(l_refs[gi])
        for t in range(t_real):
            lse_row = jnp.zeros((1, 128), F32)
            for h in range(HEADS):
                qcol = gi * HW + h * DH
                kcol = (C_GROUPS + gi) * HW + h * DH
                q = qk_ref[0, t:t + 1, qcol:qcol + DH]
                k_new = qk_ref[0, :, kcol:kcol + DH]
                vn = v_new[gi][0, :, h * DH:(h + 1) * DH]
                off = (0 if gi == 0 else t * HW) + h * DH
                k_c = kc[gi][0, 0, :, off:off + DH]
                v_c = vc[gi][0, 0, :, off:off + DH]
                sc = jnp.sum(k_c * q, axis=-1, keepdims=True) * (DH ** -0.5)
                sn = jnp.sum(k_new * q, axis=-1, keepdims=True) * (DH ** -0.5)
                if gi == 0:
                    sc = jnp.where(row_c >= t, sc, NEG)
                    sn = jnp.where(row_n <= t, sn, NEG)
                else:
                    sn = jnp.where(row_n == t, sn, NEG)
                m = jnp.maximum(jnp.max(sc, axis=0, keepdims=True), jnp.max(sn, axis=0, keepdims=True))
                pc = jnp.exp(sc - m)
                pn = jnp.exp(sn - m)
                l = jnp.sum(pc, axis=0, keepdims=True) + jnp.sum(pn, axis=0, keepdims=True)
                o = (jnp.sum(pc * v_c, axis=0, keepdims=True) + jnp.sum(pn * vn, axis=0, keepdims=True)) / l
                o_refs[gi][0, t:t + 1, h * DH:(h + 1) * DH] = o
                lse_row = jnp.where(lane == h, m + jnp.log(l), lse_row)
            l_refs[gi][0, t:t + 1, :] = lse_row


def dil_sample(qk3, proj3, caches, layer, t_real):
    bsz, tp, _ = qk3.shape
    views = []
    specs = []
    for gi in range(C_GROUPS):
        d = C_DILATIONS[gi]
        for cch in (caches[2 * gi], caches[2 * gi + 1]):
            depth, db, ln = cch.shape[:3]
            assert ln == C_WINDOWS[gi] and ln // d == C_BAND and (gi == 0 or t_real <= d)
            views.append(cch.reshape(depth, db, ln // d, d * HW))
            width = HW if gi == 0 else t_real * HW
            specs.append(pl.BlockSpec((1, 1, C_BAND, width), functools.partial(lambda b, l: (l, b, 0, 0), l=layer)))
    cv = COL_CV // HW
    in_specs = [pl.BlockSpec((1, tp, 2 * C_GROUPS * HW), lambda b: (b, 0, 0))]
    in_specs += [pl.BlockSpec((1, tp, HW), functools.partial(lambda b, k: (b, 0, k), k=cv + gi)) for gi in range(C_GROUPS)]
    in_specs += specs
    outs = pl.pallas_call(
        functools.partial(_dil_sample_kernel, t_real=t_real),
        grid=(bsz,),
        in_specs=in_specs,
        out_specs=[pl.BlockSpec((1, tp, HW), lambda b: (b, 0, 0))] * C_GROUPS
        + [pl.BlockSpec((1, tp, 128), lambda b: (b, 0, 0))] * C_GROUPS,
        out_shape=[jax.ShapeDtypeStruct((bsz, tp, HW), F32)] * C_GROUPS
        + [jax.ShapeDtypeStruct((bsz, tp, 128), F32)] * C_GROUPS,
        compiler_params=_cparams("parallel"),
        name="dil_sample",
    )(qk3, proj3, proj3, proj3, *views)
    o = [a.reshape(bsz * tp, HW) for a in outs[:C_GROUPS]]
    lse = [a.reshape(bsz * tp, 128) for a in outs[C_GROUPS:]]
    return o, lse


def _merge_kernel(x_ref, ya_ref, yb_ref, o0_ref, o1_ref, o2_ref, l0_ref, l1_ref, l2_ref,
                  g0_ref, g1_ref, g2_ref, wpa_ref, wpb_ref, wpc_ref, wo_ref, gp_ref, out_ref):
    ls = (l0_ref[...], l1_ref[...], l2_ref[...])
    os_ = (o0_ref, o1_ref, o2_ref)
    m = jnp.maximum(jnp.maximum(ls[0], ls[1]), ls[2])
    es = [jnp.exp(a - m) for a in ls]
    den = es[0] + es[1] + es[2]
    ws = [e / den for e in es]
    parts = []
    for h in range(HEADS):
        hs = slice(h * DH, (h + 1) * DH)
        acc = ws[0][:, h:h + 1] * os_[0][:, hs]
        for gi in range(1, C_GROUPS):
            acc = acc + ws[gi][:, h:h + 1] * os_[gi][:, hs]
        parts.append(acc)
    yc = jnp.concatenate(parts, axis=-1)
    merged = (_sigmoid(g0_ref[...]) * jnp.dot(ya_ref[...].astype(BF16), wpa_ref[...], preferred_element_type=F32)
              + _sigmoid(g1_ref[...]) * jnp.dot(yb_ref[...].astype(BF16), wpb_ref[...], preferred_element_type=F32)
              + _sigmoid(g2_ref[...]) * jnp.dot(yc.astype(BF16), wpc_ref[...], preferred_element_type=F32))
    z = jnp.dot(merged.astype(BF16), wo_ref[...], preferred_element_type=F32)
    out_ref[...] = x_ref[...] + _rms_rows(z, gp_ref[...])


def merge(x, ya, yb, os_, ls, proj, wpa, wpb, wpc, wo, g_post, tm):
    m = x.shape[0]
    gb = COL_GATES // D_MODEL
    row = lambda w: pl.BlockSpec((tm, w), lambda i: (i, 0))
    const = lambda a: pl.BlockSpec(a.shape, lambda i: (0, 0), pipeline_mode=pl.Buffered(1))
    in_specs = [row(D_MODEL), row(HW), row(HW), row(HW), row(HW), row(HW), row(128), row(128), row(128)]
    in_specs += [pl.BlockSpec((tm, D_MODEL), functools.partial(lambda i, k: (i, k), k=gb + n)) for n in range(3)]
    gp = g_post.reshape(1, D_MODEL)
    in_specs += [const(wpa), const(wpb), const(wpc), const(wo), const(gp)]
    return pl.pallas_call(
        _merge_kernel,
        grid=(m // tm,),
        in_specs=in_specs,
        out_specs=row(D_MODEL),
        out_shape=jax.ShapeDtypeStruct((m, D_MODEL), F32),
        compiler_params=_cparams("parallel"),
        name="merge",
    )(x, ya, yb, *os_, *ls, proj, proj, proj, wpa, wpb, wpc, wo, gp)


def _xattn_kernel(q_ref, mk_ref, mv_ref, o_ref):
    for h in range(HEADS):
        hs = slice(h * DH, (h + 1) * DH)
        sc = _dot_nt(q_ref[0, :, hs], mk_ref[0, 0, :, hs]) * (DH ** -0.5)
        m = jnp.max(sc, axis=-1, keepdims=True)
        p = jnp.exp(sc - m)
        p = p / jnp.sum(p, axis=-1, keepdims=True)
        o_ref[0, :, hs] = _dot(p, mv_ref[0, 0, :, hs])


def xattn(q3, mk4, mv4, layer, k_blk, v_blk, tm):
    bsz, t, _ = q3.shape
    return pl.pallas_call(
        _xattn_kernel,
        grid=(bsz, t // tm),
        in_specs=[pl.BlockSpec((1, tm, HW), lambda b, j: (b, j, 0)),
                  pl.BlockSpec((1, 1, N_MEM, HW), lambda b, j: (layer, b, 0, k_blk)),
                  pl.BlockSpec((1, 1, N_MEM, HW), lambda b, j: (layer, b, 0, v_blk))],
        out_specs=pl.BlockSpec((1, tm, HW), lambda b, j: (b, j, 0)),
        out_shape=jax.ShapeDtypeStruct((bsz, t, HW), F32),
        compiler_params=_cparams("parallel", "arbitrary"),
        name="xattn",
    )(q3, mk4, mv4)


def _proj_post_kernel(x_ref, o_ref, w_ref, g_ref, out_ref):
    z = jnp.dot(o_ref[...].astype(BF16), w_ref[...], preferred_element_type=F32)
    out_ref[...] = x_ref[...] + _rms_rows(z, g_ref[...])


def proj_post(x, o, w, g, tm):
    m = x.shape[0]
    k = o.shape[1]
    gp = g.reshape(1, D_MODEL)
    return pl.pallas_call(
        _proj_post_kernel,
        grid=(m // tm,),
        in_specs=[pl.BlockSpec((tm, D_MODEL), lambda i: (i, 0)),
                  pl.BlockSpec((tm, k), lambda i: (i, 0)),
                  pl.BlockSpec(w.shape, lambda i: (0, 0), pipeline_mode=pl.Buffered(1)),
                  pl.BlockSpec((1, D_MODEL), lambda i: (0, 0), pipeline_mode=pl.Buffered(1))],
        out_specs=pl.BlockSpec((tm, D_MODEL), lambda i: (i, 0)),
        out_shape=jax.ShapeDtypeStruct((m, D_MODEL), F32),
        compiler_params=_cparams("parallel"),
        name="proj_post",
    )(x, o, w, gp)


def _ffn_kernel(*refs, tm, use_ovr):
    if use_ovr:
        (x_ref, gpre_ref, wup_ref, cw_ref, cb_ref, wdn_ref, gpost_ref, buf_ref, ovr_ref,
         out_ref, up_ref, carry_ref, win_ref) = refs
    else:
        (x_ref, gpre_ref, wup_ref, cw_ref, cb_ref, wdn_ref, gpost_ref, buf_ref,
         out_ref, up_ref, carry_ref, win_ref) = refs
    kw = FFN_CONV - 1

    @pl.when(pl.program_id(1) == 0)
    def _():
        carry_ref[8 - kw:8, :] = buf_ref[0]

    x = x_ref[0]
    h = _rms_rows(x, gpre_ref[...]).astype(BF16)
    if use_ovr:
        slot_row = lax.broadcasted_iota(jnp.int32, (tm, 1), 0) % SAMPLE_T_PAD
        is_ovr = slot_row >= SAMPLE_T_PAD - kw
    acc = jnp.zeros((tm, D_MODEL), F32)
    for cidx in range(D_FF // FFN_COLS):
        ys = []
        for half in range(2):
            c0 = half * D_FF + cidx * FFN_COLS
            cols = slice(c0, c0 + FFN_COLS)
            u = jnp.dot(h, wup_ref[:, cols], preferred_element_type=F32)
            if use_ovr:
                u = jnp.where(is_ovr, ovr_ref[0, :, cols], u)
            up_ref[0, :, cols] = u
            win_ref[8 - kw:8, :] = carry_ref[8 - kw:8, cols]
            win_ref[8:8 + tm, :] = u
            y = u * cw_ref[kw:kw + 1, cols] + cb_ref[:, cols]
            for i in range(kw):
                y = y + win_ref[8 - kw + i:8 - kw + i + tm, :] * cw_ref[i:i + 1, cols]
            carry_ref[8 - kw:8, cols] = win_ref[8 + tm - kw:8 + tm, :]
            ys.append(y)
        y1, y2 = ys
        gelu = 0.5 * y1 * (1.0 + jnp.tanh(math.sqrt(2.0 / math.pi) * (y1 + 0.044715 * (y1 * y1 * y1))))
        a = (gelu * y2).astype(BF16)
        acc = acc + jnp.dot(a, wdn_ref[cidx * FFN_COLS:(cidx + 1) * FFN_COLS, :], preferred_element_type=F32)
    out_ref[0] = x + _rms_rows(acc, gpost_ref[...])


def conv_ffn(x3, g_pre, w_up, cw, cb, w_down, g_post, buf0, ovr3, tm):
    bsz, t, _ = x3.shape
    use_ovr = ovr3 is not None
    const = lambda a: pl.BlockSpec(a.shape, lambda b, j: (0,) * a.ndim, pipeline_mode=pl.Buffered(1))
    gpre = g_pre.reshape(1, D_MODEL)
    gpost = g_post.reshape(1, D_MODEL)
    cb2 = cb.reshape(1, 2 * D_FF)
    in_specs = [pl.BlockSpec((1, tm, D_MODEL), lambda b, j: (b, j, 0)),
                const(gpre), const(w_up), const(cw), const(cb2), const(w_down), const(gpost),
                pl.BlockSpec((1, FFN_CONV - 1, 2 * D_FF), lambda b, j: (b, 0, 0))]
    args = [x3, gpre, w_up, cw, cb2, w_down, gpost, buf0]
    if use_ovr:
        in_specs.append(pl.BlockSpec((1, tm, 2 * D_FF), lambda b, j: (b, j, 0)))
        args.append(ovr3)
    return pl.pallas_call(
        functools.partial(_ffn_kernel, tm=tm, use_ovr=use_ovr),
        grid=(bsz, t // tm),
        in_specs=in_specs,
        out_specs=[pl.BlockSpec((1, tm, D_MODEL), lambda b, j: (b, j, 0)),
                   pl.BlockSpec((1, tm, 2 * D_FF), lambda b, j: (b, j, 0))],
        out_shape=[jax.ShapeDtypeStruct((bsz, t, D_MODEL), F32),
                   jax.ShapeDtypeStruct((bsz, t, 2 * D_FF), F32)],
        scratch_shapes=[pltpu.VMEM((8, 2 * D_FF), F32), pltpu.VMEM((tm + 8, FFN_COLS), F32)],
        compiler_params=_cparams("parallel", "arbitrary"),
        name="conv_ffn",
    )(*args)


def _pick_tile(m, pref):
    t = min(m, pref)
    assert m % t == 0
    return t


def _mixer_and_ffn(x, bsz, t, t_real, W, sa, sb, sbc, caches, layer, sf, mk4, mv4, mem_layer, k_blk, v_blk,
                   cos_t, sin_t):
    m = bsz * t
    prompt = caches is None
    proj = rms_matmul(x, W['g_mix_pre'], W['w_in'], _pick_tile(m, 1024), 1024)
    proj3 = proj.reshape(bsz, t, N_PROJ)
    c_rec, tb_rec = (16, 256) if prompt else (SAMPLE_T_PAD, SAMPLE_T_PAD)
    ya, sa_new = hgrn2(proj3, W['lb'], W['a_onorm'], sa, c_rec, tb_rec, t_real)
    yb, sb_new, sbc_new = gdn(proj3, W['b_conv_w'], sbc, W['alog_row'], W['dt_row'], W['b_onorm'], sb,
                              c_rec, tb_rec, t_real)
    qk = rope_qk(proj, cos_t, sin_t, _pick_tile(m, 512))
    qk3 = qk.reshape(bsz, t, 2 * C_GROUPS * HW)
    if prompt:
        os_, ls = [], []
        for gi in range(C_GROUPS):
            o, lse = dil_prompt(qk3, proj3, gi)
            os_.append(o)
            ls.append(lse)
    else:
        os_, ls = dil_sample(qk3, proj3, caches, layer, t_real)
    rows = []
    for gi in range(C_GROUPS):
        keep = min(C_WINDOWS[gi], t_real)
        kcol = (C_GROUPS + gi) * HW
        vcol = COL_CV + gi * HW
        rows.append(qk3[:, t_real - keep:t_real, kcol:kcol + HW].reshape(bsz, keep, HEADS, DH))
        rows.append(proj3[:, t_real - keep:t_real, vcol:vcol + HW].reshape(bsz, keep, HEADS, DH))
    tm = _pick_tile(m, 256)
    x = merge(x, ya.reshape(m, HW), yb.reshape(m, HW), os_, ls, proj,
              W['w_pa'], W['w_pb'], W['w_pc'], W['w_o'], W['g_mix_post'], tm)
    q = rms_matmul(x, W['g_x_pre'], W['w_xq'], _pick_tile(m, 1024), HW)
    o = xattn(q.reshape(bsz, t, HW), mk4, mv4, mem_layer, k_blk, v_blk, _pick_tile(t, 256))
    x = proj_post(x, o.reshape(m, HW), W['w_xo'], W['g_x_post'], tm)
    if prompt:
        xo, up = conv_ffn(x.reshape(bsz, t, D_MODEL), W['g_ffn_pre'], W['w_up'], W['ffn_conv_w'], W['ffn_conv_b'],
                          W['w_down'], W['g_ffn_post'], sf, None, tm)
        sf_new = up[:, t_real - (FFN_CONV - 1):t_real]
    else:
        kw = FFN_CONV - 1
        nxt = jnp.concatenate([sf[1:], sf[:1]], axis=0)
        ovr = jnp.concatenate([jnp.zeros((bsz, t - kw, 2 * D_FF), F32), nxt], axis=1).reshape(1, m, 2 * D_FF)
        xo, up = conv_ffn(x.reshape(1, m, D_MODEL), W['g_ffn_pre'], W['w_up'], W['ffn_conv_w'], W['ffn_conv_b'],
                          W['w_down'], W['g_ffn_post'], sf[:1], ovr, tm)
        sf_new = up.reshape(bsz, t, 2 * D_FF)[:, t_real - kw:t_real]
    return xo.reshape(m, D_MODEL), sa_new, sb_new, sbc_new, rows, sf_new


def _rope_tables(pos):
    half = DH // 2
    inv = ROPE_THETA ** (-jnp.arange(half, dtype=F32) / half)
    ang = pos.astype(F32)[:, None] * inv[None, :]
    cos, sin = jnp.cos(ang), jnp.sin(ang)
    return jnp.concatenate([cos, cos], axis=-1), jnp.concatenate([-sin, sin], axis=-1)


def _reorder_w_in(w):
    o = 4 * HW
    a = w[:, :o]
    bqkv = w[:, o:o + B_CONV_CH]
    o += B_CONV_CH
    bz = w[:, o:o + HW]
    o += HW
    bba = w[:, o:o + 2 * HEADS]
    o += 2 * HEADS
    c = w[:, o:o + 3 * C_GROUPS * HW]
    o += 3 * C_GROUPS * HW
    gates = w[:, o:]
    pad = jnp.zeros((w.shape[0], HW - 2 * HEADS), w.dtype)
    out = jnp.concatenate([bqkv, bz, a, c, bba, pad, gates], axis=1).astype(BF16)
    assert out.shape[1] == N_PROJ
    return out


def kernel(x_prompt, x_sample, mem_prompt, state_a, state_b, state_b_conv, cache_c0_k, cache_c0_v, cache_c1_k,
           cache_c1_v, cache_c2_k, cache_c2_v, state_ffn_conv, cache_mem_k, cache_mem_v, g_mix_pre, g_mix_post,
           g_x_pre, g_x_post, g_mem, g_ffn_pre, g_ffn_post, w_in, a_lb, a_onorm, b_conv_w, b_a_log, b_dt_bias,
           b_onorm, w_pa, w_pb, w_pc, w_o, w_xq, w_xk, w_xv, w_xo, w_up, ffn_conv_w, ffn_conv_b, w_down):
    depth = w_in.shape[0]
    bp, tp, _ = x_prompt.shape
    bs, ts, _ = x_sample.shape
    tsp = SAMPLE_T_PAD
    sm = jax.nn.softmax(a_lb.astype(F32), axis=0)
    lb_all = jnp.cumsum(sm, axis=0) - sm[0]
    cos_p, sin_p = _rope_tables(jnp.arange(tp, dtype=jnp.int32))
    cos_p, sin_p = jnp.tile(cos_p, (bp, 1)), jnp.tile(sin_p, (bp, 1))
    cos_s, sin_s = _rope_tables(PAST_LEN + jnp.arange(tsp, dtype=jnp.int32))
    cos_s, sin_s = jnp.tile(cos_s, (bs, 1)), jnp.tile(sin_s, (bs, 1))
    caches = tuple(c.reshape(c.shape[0], c.shape[1], c.shape[2], HW)
                   for c in (cache_c0_k, cache_c0_v, cache_c1_k, cache_c1_v, cache_c2_k, cache_c2_v))
    cmk = cache_mem_k.reshape(depth, bs, N_MEM, HW)
    cmv = cache_mem_v.reshape(depth, bs, N_MEM, HW)

    xp = x_prompt.reshape(bp * tp, D_MODEL)
    xs = jnp.pad(x_sample, ((0, 0), (0, tsp - ts), (0, 0))).reshape(bs * tsp, D_MODEL)
    zeros_state = jnp.zeros((bp, HEADS, DH, DH), F32)
    zeros_bconv = jnp.zeros((bp, B_CONV - 1, B_CONV_CH), F32)
    zeros_fconv = jnp.zeros((bp, FFN_CONV - 1, 2 * D_FF), F32)
    lane8 = jnp.zeros((1, 128), F32)

    po = [[] for _ in range(12)]
    so = [[] for _ in range(10)]
    for l in range(depth):
        W = dict(g_mix_pre=g_mix_pre[l], g_mix_post=g_mix_post[l], g_x_pre=g_x_pre[l], g_x_post=g_x_post[l],
                 g_ffn_pre=g_ffn_pre[l], g_ffn_post=g_ffn_post[l], w_in=_reorder_w_in(w_in[l]), lb=lb_all[l],
                 a_onorm=a_onorm[l], b_conv_w=b_conv_w[l],
                 alog_row=lane8.at[0, HEADS:2 * HEADS].set(b_a_log[l]),
                 dt_row=lane8.at[0, HEADS:2 * HEADS].set(b_dt_bias[l]),
                 b_onorm=b_onorm[l], w_pa=w_pa[l].astype(BF16), w_pb=w_pb[l].astype(BF16),
                 w_pc=w_pc[l].astype(BF16), w_o=w_o[l].astype(BF16), w_xq=w_xq[l].astype(BF16),
                 w_xo=w_xo[l].astype(BF16), w_up=w_up[l].astype(BF16), ffn_conv_w=ffn_conv_w[l],
                 ffn_conv_b=ffn_conv_b[l], w_down=w_down[l].astype(BF16))
        w_kv = jnp.concatenate([w_xk[l], w_xv[l]], axis=1).astype(BF16)
        mkv = rms_matmul(mem_prompt.reshape(bp * N_MEM, D_MODEL), g_mem[l], w_kv, bp * N_MEM, HW)
        mkv4 = mkv.reshape(1, bp, N_MEM, 2 * HW)
        xp, sa, sb, sbc, rows, sf = _mixer_and_ffn(
            xp, bp, tp, tp, W, zeros_state, zeros_state, zeros_bconv, None, l, zeros_fconv,
            mkv4, mkv4, 0, 0, 1, cos_p, sin_p)
        mk = mkv4[0, :, :, :HW].reshape(bp, N_MEM, HEADS, DH)
        mv = mkv4[0, :, :, HW:].reshape(bp, N_MEM, HEADS, DH)
        for i, a in enumerate([sa, sb, sbc] + rows + [sf, mk, mv]):
            po[i].append(a)
        xs, sa, sb, sbc, rows, sf = _mixer_and_ffn(
            xs, bs, tsp, ts, W, state_a[l], state_b[l], state_b_conv[l], caches, l, state_ffn_conv[l],
            cmk, cmv, l, 0, 0, cos_s, sin_s)
        for i, a in enumerate([sa, sb, sbc] + rows + [sf]):
            so[i].append(a)
    p_out = [jnp.stack(a, axis=0) for a in po]
    s_out = [jnp.stack(a, axis=0) for a in so]
    y_prompt = xp.reshape(bp, tp, D_MODEL)
    y_sample = xs.reshape(bs, tsp, D_MODEL)[:, :ts]
    return tuple([y_prompt, y_sample] + p_out + s_out)
```

```python
import functools
import math

import jax
import jax.numpy as jnp
from jax import lax
from jax.experimental import pallas as pl
from jax.experimental.pallas import tpu as pltpu

F32 = jnp.float32
BF16 = jnp.bfloat16

D_MODEL = 1024
PAST_LEN = 2048
EPS = 1e-6
NEG = -1e30
LB_FLOOR = 1e-30
ROPE_THETA = 10000.0
HEADS = 4
DH = 128
HW = HEADS * DH
B_CONV = 4
B_CONV_CH = 3 * HW
C_WINDOWS = (128, 512, 2048)
C_DILATIONS = (1, 4, 16)
C_GROUPS = 3
C_BAND = 128
C_BLK = 128
N_MEM = 256
D_FF = 128 * ((8 * D_MODEL // 3 + 127) // 128)
FFN_CONV = 3
FFN_COLS = 256
SAMPLE_T_PAD = 8

COL_BQKV = 0
COL_BZ = 1536
COL_A = 2048
COL_CQ = 4096
COL_CK = 5632
COL_CV = 7168
COL_BBA = 8704
COL_GATES = 9216
N_PROJ = 12288

VMEM_LIMIT_BYTES = 56 * 1024 * 1024


def _cparams(*sem):
    return pltpu.CompilerParams(dimension_semantics=sem, vmem_limit_bytes=VMEM_LIMIT_BYTES)


def _dot(a, b):
    return jnp.dot(a.astype(BF16), b.astype(BF16), preferred_element_type=F32)


def _dot_nt(a, b):
    return lax.dot_general(a.astype(BF16), b.astype(BF16), (((1,), (1,)), ((), ())), preferred_element_type=F32)


def _dot_tn(a, b):
    return lax.dot_general(a.astype(BF16), b.astype(BF16), (((0,), (0,)), ((), ())), preferred_element_type=F32)


def _cumsum_rows(tri, x):
    return jnp.dot(tri, x, precision=lax.Precision.HIGHEST, preferred_element_type=F32)


def _rms_rows(x, g):
    return x * lax.rsqrt(jnp.mean(x * x, axis=-1, keepdims=True) + EPS) * g


def _sigmoid(x):
    return jax.nn.sigmoid(x)


def _silu(x):
    return x * jax.nn.sigmoid(x)


def _log1p_exp_neg_abs(x):
    return jnp.log1p(jnp.exp(-jnp.abs(x)))


def _tri(c):
    r = lax.broadcasted_iota(jnp.int32, (c, c), 0)
    s = lax.broadcasted_iota(jnp.int32, (c, c), 1)
    return r, s


def _rms_matmul_kernel(x_ref, g_ref, w_ref, o_ref, xn_ref):
    @pl.when(pl.program_id(1) == 0)
    def _():
        xn_ref[...] = _rms_rows(x_ref[...], g_ref[...]).astype(BF16)

    o_ref[...] = jnp.dot(xn_ref[...], w_ref[...], preferred_element_type=F32)


def rms_matmul(x, g, w, tm, tn):
    m, k = x.shape
    n = w.shape[1]
    return pl.pallas_call(
        _rms_matmul_kernel,
        grid=(m // tm, n // tn),
        in_specs=[pl.BlockSpec((tm, k), lambda i, j: (i, 0)),
                  pl.BlockSpec((1, k), lambda i, j: (0, 0)),
                  pl.BlockSpec((k, tn), lambda i, j: (0, j))],
        out_specs=pl.BlockSpec((tm, tn), lambda i, j: (i, j)),
        out_shape=jax.ShapeDtypeStruct((m, n), F32),
        scratch_shapes=[pltpu.VMEM((tm, k), BF16)],
        compiler_params=_cparams("parallel", "arbitrary"),
        name="rms_matmul",
    )(x, g.reshape(1, k), w)


def _hgrn2_kernel(pq_ref, pf_ref, pi_ref, pg_ref, lb_ref, on_ref, s0_ref, y_ref, s_ref, *, c, tb, t_real):
    j = pl.program_id(1)
    n_j = pl.num_programs(1)

    @pl.when(j == 0)
    def _():
        for h in range(HEADS):
            s_ref[0, h] = s0_ref[0, h].T

    lb = lb_ref[...]
    log_lb = jnp.log(jnp.maximum(lb, LB_FLOOR))
    log_1mlb = jnp.log1p(-lb)
    onorm = on_ref[...]
    r_i, s_i = _tri(c)
    tri = (r_i >= s_i).astype(F32)
    row = lax.broadcasted_iota(jnp.int32, (c, 1), 0)
    n_diag = min(c, t_real)

    def chunk(i, carry):
        r0 = pl.multiple_of(i * c, c)
        rows = pl.ds(r0, c)
        zq = pq_ref[0, rows, :]
        zf = pf_ref[0, rows, :]
        v = pi_ref[0, rows, :]
        zg = pg_ref[0, rows, :]
        q = _silu(zq) * (DH ** -0.5)
        log_sig = jnp.minimum(zf, 0.0) - _log1p_exp_neg_abs(zf)
        t1 = log_1mlb + log_sig
        logf = jnp.maximum(log_lb, t1) + _log1p_exp_neg_abs(log_lb - t1)
        k = -jnp.tanh(0.5 * logf) * (jnp.exp(logf) + 1.0)
        if t_real < tb:
            live = (j * tb + r0 + row) < t_real
            logf = jnp.where(live, logf, 0.0)
            k = jnp.where(live, k, 0.0)
        b = _cumsum_rows(tri, logf)
        bl = b[c - 1:c, :]
        qs = q * jnp.exp(b)
        kd = k * jnp.exp(bl - b)
        el = jnp.exp(bl)
        for h in range(HEADS):
            hs = slice(h * DH, (h + 1) * DH)
            st = s_ref[0, h]
            o = _dot_nt(qs[:, hs], st)
            bh, qh, kh, vh = b[:, hs], q[:, hs], k[:, hs], v[:, hs]
            for s in range(n_diag):
                m = row >= s
                d = jnp.where(m, bh - bh[s:s + 1, :], 0.0)
                a = qh * kh[s:s + 1, :] * jnp.exp(d)
                col = jnp.where(m, jnp.sum(a, axis=-1, keepdims=True), 0.0)
                o = o + col * vh[s:s + 1, :]
            s_ref[0, h] = st * el[:, hs] + _dot_tn(vh, kd[:, hs])
            y_ref[0, rows, hs] = _rms_rows(o, onorm) * _silu(zg[:, hs])
        return carry

    lax.fori_loop(0, tb // c, chunk, 0)

    @pl.when(j == n_j - 1)
    def _():
        for h in range(HEADS):
            s_ref[0, h] = s_ref[0, h].T


def hgrn2(proj3, lb, onorm, s0, c, tb, t_real):
    bsz, t, _ = proj3.shape
    cb = COL_A // HW
    in_specs = [pl.BlockSpec((1, tb, HW), functools.partial(lambda b, j, k: (b, j, k), k=cb + i)) for i in range(4)]
    in_specs += [pl.BlockSpec((1, HW), lambda b, j: (0, 0)),
                 pl.BlockSpec((1, DH), lambda b, j: (0, 0)),
                 pl.BlockSpec((1, HEADS, DH, DH), lambda b, j: (b, 0, 0, 0))]
    return pl.pallas_call(
        functools.partial(_hgrn2_kernel, c=c, tb=tb, t_real=t_real),
        grid=(bsz, t // tb),
        in_specs=in_specs,
        out_specs=[pl.BlockSpec((1, tb, HW), lambda b, j: (b, j, 0)),
                   pl.BlockSpec((1, HEADS, DH, DH), lambda b, j: (b, 0, 0, 0))],
        out_shape=[jax.ShapeDtypeStruct((bsz, t, HW), F32),
                   jax.ShapeDtypeStruct((bsz, HEADS, DH, DH), F32)],
        compiler_params=_cparams("parallel", "arbitrary"),
        name="hgrn2",
    )(proj3, proj3, proj3, proj3, lb.reshape(1, HW), onorm.reshape(1, DH), s0)


def _gdn_kernel(x_ref, z_ref, ba_ref, cw_ref, buf_ref, alog_ref, dt_ref, on_ref, s0_ref,
                y_ref, s_ref, nbuf_ref, win_ref, act_ref, *, c, tb, t_real):
    j = pl.program_id(1)
    n_j = pl.num_programs(1)
    kw = B_CONV - 1

    @pl.when(j == 0)
    def _():
        win_ref[8 - kw:8, :] = buf_ref[0]
        s_ref[...] = s0_ref[...]

    win_ref[8:8 + tb, :] = x_ref[0]
    conv = win_ref[8 - kw:8 - kw + tb, :] * cw_ref[0:1, :]
    for i in range(1, B_CONV):
        conv = conv + win_ref[8 - kw + i:8 - kw + i + tb, :] * cw_ref[i:i + 1, :]
    act_ref[...] = _silu(conv)

    @pl.when(j == n_j - 1)
    def _():
        last = t_real - (t_real - 1) // tb * tb
        nbuf_ref[0] = win_ref[8 + last - kw:8 + last, :]

    win_ref[8 - kw:8, :] = win_ref[8 + tb - kw:8 + tb, :]

    onorm = on_ref[...]
    r_i, s_i = _tri(c)
    tri = (r_i >= s_i).astype(F32)
    strict = r_i > s_i
    incl = r_i >= s_i
    row = lax.broadcasted_iota(jnp.int32, (c, 1), 0)

    def chunk(i, carry):
        r0 = pl.multiple_of(i * c, c)
        rows = pl.ds(r0, c)
        blk = ba_ref[0, rows, :]
        beta_all = _sigmoid(blk)
        sp_in = blk + dt_ref[...]
        softplus = jnp.maximum(sp_in, 0.0) + _log1p_exp_neg_abs(sp_in)
        g_all = -jnp.exp(alog_ref[...]) * softplus
        if t_real < tb:
            live = (j * tb + r0 + row) < t_real
            beta_all = jnp.where(live, beta_all, 0.0)
            g_all = jnp.where(live, g_all, 0.0)
        bg = _cumsum_rows(tri, g_all)
        bg_t = bg.T
        z = z_ref[0, rows, :]
        qss, qks, kds, bls, lows, rhs = [], [], [], [], [], []
        for h in range(HEADS):
            qa = act_ref[rows, h * DH:(h + 1) * DH]
            ka = act_ref[rows, HW + h * DH:HW + (h + 1) * DH]
            vh = act_ref[rows, 2 * HW + h * DH:2 * HW + (h + 1) * DH]
            qh = qa * lax.rsqrt(jnp.sum(qa * qa, axis=-1, keepdims=True) + EPS) * (DH ** -0.5)
            kh = ka * lax.rsqrt(jnp.sum(ka * ka, axis=-1, keepdims=True) + EPS)
            beta = beta_all[:, h:h + 1]
            b_col = bg[:, HEADS + h:HEADS + h + 1]
            b_row = bg_t[HEADS + h:HEADS + h + 1, :]
            diff = b_col - b_row
            dec_s = jnp.where(strict, jnp.exp(jnp.where(strict, diff, 0.0)), 0.0)
            dec_i = jnp.where(incl, jnp.exp(jnp.where(incl, diff, 0.0)), 0.0)
            st = s_ref[0, h]
            eb = jnp.exp(b_col)
            bl = b_col[c - 1:c, :]
            lows.append(beta * _dot_nt(kh, kh) * dec_s)
            rhs.append(beta * (vh - eb * _dot(kh, st)))
            qss.append(eb * _dot(qh, st))
            qks.append(_dot_nt(qh, kh) * dec_i)
            kds.append(kh * jnp.exp(bl - b_col))
            bls.append(bl)
        u = jnp.concatenate(rhs, axis=1)
        for t in range(c - 1):
            lcol = jnp.concatenate([jnp.broadcast_to(lw[:, t:t + 1], (c, DH)) for lw in lows], axis=1)
            u = u - lcol * u[t:t + 1, :]
        for h in range(HEADS):
            hs = slice(h * DH, (h + 1) * DH)
            uh = u[:, hs]
            o = qss[h] + _dot(qks[h], uh)
            s_ref[0, h] = s_ref[0, h] * jnp.exp(bls[h]) + _dot_tn(kds[h], uh)
            y_ref[0, rows, hs] = _rms_rows(o, onorm) * _silu(z[:, hs])
        return carry

    lax.fori_loop(0, tb // c, chunk, 0)


def gdn(proj3, conv_w, conv_buf, alog_row, dt_row, onorm, s0, c, tb, t_real):
    bsz, t, _ = proj3.shape
    assert t_real - (t_real - 1) // tb * tb >= B_CONV - 1
    return pl.pallas_call(
        functools.partial(_gdn_kernel, c=c, tb=tb, t_real=t_real),
        grid=(bsz, t // tb),
        in_specs=[pl.BlockSpec((1, tb, B_CONV_CH), lambda b, j: (b, j, COL_BQKV // B_CONV_CH)),
                  pl.BlockSpec((1, tb, HW), lambda b, j: (b, j, COL_BZ // HW)),
                  pl.BlockSpec((1, tb, 128), lambda b, j: (b, j, COL_BBA // 128)),
                  pl.BlockSpec((B_CONV, B_CONV_CH), lambda b, j: (0, 0)),
                  pl.BlockSpec((1, B_CONV - 1, B_CONV_CH), lambda b, j: (b, 0, 0)),
                  pl.BlockSpec((1, 128), lambda b, j: (0, 0)),
                  pl.BlockSpec((1, 128), lambda b, j: (0, 0)),
                  pl.BlockSpec((1, DH), lambda b, j: (0, 0)),
                  pl.BlockSpec((1, HEADS, DH, DH), lambda b, j: (b, 0, 0, 0))],
        out_specs=[pl.BlockSpec((1, tb, HW), lambda b, j: (b, j, 0)),
                   pl.BlockSpec((1, HEADS, DH, DH), lambda b, j: (b, 0, 0, 0)),
                   pl.BlockSpec((1, B_CONV - 1, B_CONV_CH), lambda b, j: (b, 0, 0))],
        out_shape=[jax.ShapeDtypeStruct((bsz, t, HW), F32),
                   jax.ShapeDtypeStruct((bsz, HEADS, DH, DH), F32),
                   jax.ShapeDtypeStruct((bsz, B_CONV - 1, B_CONV_CH), F32)],
        scratch_shapes=[pltpu.VMEM((tb + 8, B_CONV_CH), F32), pltpu.VMEM((tb, B_CONV_CH), F32)],
        compiler_params=_cparams("parallel", "arbitrary"),
        name="gdn",
    )(proj3, proj3, proj3, conv_w, conv_buf, alog_row, dt_row, onorm.reshape(1, DH), s0)


QKV_PLANES = 3 * C_GROUPS * HEADS


def _rope_kernel(x_ref, cos_ref, sin_ref, o_ref):
    j = pl.program_id(1)

    @pl.when(j < 2 * C_GROUPS)
    def _():
        cs = cos_ref[...]
        sn = sin_ref[...]
        for h in range(HEADS):
            x = x_ref[:, h * DH:(h + 1) * DH]
            o_ref[h] = x * cs + pltpu.roll(x, DH // 2, 1) * sn

    @pl.when(j >= 2 * C_GROUPS)
    def _():
        for h in range(HEADS):
            o_ref[h] = x_ref[:, h * DH:(h + 1) * DH]


def rope_qkv(proj, cos_t, sin_t, tm):
    m = proj.shape[0]
    cb = COL_CQ // HW
    return pl.pallas_call(
        _rope_kernel,
        grid=(m // tm, 3 * C_GROUPS),
        in_specs=[pl.BlockSpec((tm, HW), lambda i, j: (i, cb + j)),
                  pl.BlockSpec((tm, DH), lambda i, j: (i, 0)),
                  pl.BlockSpec((tm, DH), lambda i, j: (i, 0))],
        out_specs=pl.BlockSpec((HEADS, tm, DH), lambda i, j: (j, i, 0)),
        out_shape=jax.ShapeDtypeStruct((QKV_PLANES, m, DH), F32),
        compiler_params=_cparams("parallel", "arbitrary"),
        name="rope_qkv",
    )(proj, cos_t, sin_t)


DIL_ROWS = C_BLK * max(C_DILATIONS)


def _dil_prompt_kernel(q_ref, kc_ref, vc_ref, kp_ref, vp_ref, o_ref, lse_ref, *, d, t_blocks):
    first = (pl.program_id(0) % t_blocks) == 0
    span = C_BLK * d
    qi, ki = _tri(C_BLK)
    in_band = ki >= qi
    mask_cur = ki <= qi
    for blk in range(DIL_ROWS // span):
        for r in range(d):
            rows = pl.ds(blk * span + r, C_BLK, stride=d)
            q = q_ref[0, rows, :]
            if blk > 0:
                prows = pl.ds((blk - 1) * span + r, C_BLK, stride=d)
                k_prev, v_prev, mask_prev = kc_ref[0, prows, :], vc_ref[0, prows, :], in_band
            else:
                prows = pl.ds(r, C_BLK, stride=d)
                k_prev, v_prev = kp_ref[0, prows, :], vp_ref[0, prows, :]
                mask_prev = in_band & jnp.logical_not(first)
            sp = jnp.where(mask_prev, _dot_nt(q, k_prev) * (DH ** -0.5), NEG)
            sc = jnp.where(mask_cur, _dot_nt(q, kc_ref[0, rows, :]) * (DH ** -0.5), NEG)
            m = jnp.maximum(jnp.max(sp, axis=-1, keepdims=True), jnp.max(sc, axis=-1, keepdims=True))
            pp = jnp.exp(sp - m)
            pc = jnp.exp(sc - m)
            l = jnp.sum(pp, axis=-1, keepdims=True) + jnp.sum(pc, axis=-1, keepdims=True)
            o_ref[rows, :] = (_dot(pp, v_prev) + _dot(pc, vc_ref[0, rows, :])) / l
            lse_ref[rows, :] = jnp.broadcast_to(m + jnp.log(l), (C_BLK, DH))


def dil_prompt(qkv, t, gi):
    m = qkv.shape[1]
    d = C_DILATIONS[gi]
    span = C_BLK * d
    r_blk = DIL_ROWS
    assert t % r_blk == 0 and r_blk % span == 0
    per = r_blk // span
    nh = C_GROUPS * HEADS

    def cur(part):
        return pl.BlockSpec((1, r_blk, DH), lambda i, h: (part * nh + gi * HEADS + h, i, 0))

    def prev(part):
        return pl.BlockSpec((1, span, DH), lambda i, h: (part * nh + gi * HEADS + h, jnp.maximum(i * per - 1, 0), 0))

    return pl.pallas_call(
        functools.partial(_dil_prompt_kernel, d=d, t_blocks=t // r_blk),
        grid=(m // r_blk, HEADS),
        in_specs=[cur(0), cur(1), cur(2), prev(1), prev(2)],
        out_specs=[pl.BlockSpec((r_blk, DH), lambda i, h: (i, h))] * 2,
        out_shape=[jax.ShapeDtypeStruct((m, HW), F32)] * 2,
        compiler_params=_cparams("parallel", "arbitrary"),
        name="dil_prompt_g%d" % gi,
    )(qkv, qkv, qkv, qkv, qkv)


def _dil_sample_kernel(qkv_ref, k0c, v0c, k1c, v1c, k2c, v2c,
                       o0_ref, o1_ref, o2_ref, l0_ref, l1_ref, l2_ref, *, t_real):
    kc = (k0c, k1c, k2c)
    vc = (v0c, v1c, v2c)
    o_refs = (o0_ref, o1_ref, o2_ref)
    l_refs = (l0_ref, l1_ref, l2_ref)
    tp = SAMPLE_T_PAD
    nh = C_GROUPS * HEADS
    row_c = lax.broadcasted_iota(jnp.int32, (C_BAND, 1), 0)
    row_n = lax.broadcasted_iota(jnp.int32, (tp, 1), 0)
    for gi in range(C_GROUPS):
        o_refs[gi][...] = jnp.zeros_like(o_refs[gi])
        l_refs[gi][...] = jnp.zeros_like(l_refs[gi])
        for t in range(t_real):
            for h in range(HEADS):
                hs = slice(h * DH, (h + 1) * DH)
                q = qkv_ref[gi * HEADS + h, t:t + 1, :]
                k_new = qkv_ref[nh + gi * HEADS + h]
                vn = qkv_ref[2 * nh + gi * HEADS + h]
                if gi == 0:
                    k_c, v_c = kc[gi][0, 0, :, h, :], vc[gi][0, 0, :, h, :]
                else:
                    k_c, v_c = kc[gi][0, 0, :, t, h, :], vc[gi][0, 0, :, t, h, :]
                sc = jnp.sum(k_c * q, axis=-1, keepdims=True) * (DH ** -0.5)
                sn = jnp.sum(k_new * q, axis=-1, keepdims=True) * (DH ** -0.5)
                if gi == 0:
                    sc = jnp.where(row_c >= t, sc, NEG)
                    sn = jnp.where(row_n <= t, sn, NEG)
                else:
                    sn = jnp.where(row_n == t, sn, NEG)
                m = jnp.maximum(jnp.max(sc, axis=0, keepdims=True), jnp.max(sn, axis=0, keepdims=True))
                pc = jnp.exp(sc - m)
                pn = jnp.exp(sn - m)
                l = jnp.sum(pc, axis=0, keepdims=True) + jnp.sum(pn, axis=0, keepdims=True)
                o = (jnp.sum(pc * v_c, axis=0, keepdims=True) + jnp.sum(pn * vn, axis=0, keepdims=True)) / l
                o_refs[gi][0, t:t + 1, hs] = o
                l_refs[gi][0, t:t + 1, hs] = jnp.broadcast_to(m + jnp.log(l), (1, DH))


def dil_sample(qkv, bsz, caches, layer, t_real):
    tp = SAMPLE_T_PAD
    views = []
    specs = []
    for gi in range(C_GROUPS):
        d = C_DILATIONS[gi]
        for cch in (caches[2 * gi], caches[2 * gi + 1]):
            depth, db, ln = cch.shape[:3]
            assert ln == C_WINDOWS[gi] and ln // d == C_BAND and (gi == 0 or t_real <= d)
            if gi == 0:
                views.append(cch)
                specs.append(pl.BlockSpec((1, 1, C_BAND, HEADS, DH),
                                          functools.partial(lambda b, l: (l, b, 0, 0, 0), l=layer)))
            else:
                views.append(cch.reshape(depth, db, C_BAND, d, HEADS, DH))
                specs.append(pl.BlockSpec((1, 1, C_BAND, t_real, HEADS, DH),
                                          functools.partial(lambda b, l: (l, b, 0, 0, 0, 0), l=layer)))
    outs = pl.pallas_call(
        functools.partial(_dil_sample_kernel, t_real=t_real),
        grid=(bsz,),
        in_specs=[pl.BlockSpec((QKV_PLANES, tp, DH), lambda b: (0, b, 0))] + specs,
        out_specs=[pl.BlockSpec((1, tp, HW), lambda b: (b, 0, 0))] * (2 * C_GROUPS),
        out_shape=[jax.ShapeDtypeStruct((bsz, tp, HW), F32)] * (2 * C_GROUPS),
        compiler_params=_cparams("parallel"),
        name="dil_sample",
    )(qkv, *views)
    o = [a.reshape(bsz * tp, HW) for a in outs[:C_GROUPS]]
    lse = [a.reshape(bsz * tp, HW) for a in outs[C_GROUPS:]]
    return o, lse


def _merge_kernel(x_ref, ya_ref, yb_ref, o0_ref, o1_ref, o2_ref, l0_ref, l1_ref, l2_ref,
                  g0_ref, g1_ref, g2_ref, wpa_ref, wpb_ref, wpc_ref, wo_ref, gp_ref, out_ref):
    ls = (l0_ref[...], l1_ref[...], l2_ref[...])
    m = jnp.maximum(jnp.maximum(ls[0], ls[1]), ls[2])
    es = [jnp.exp(a - m) for a in ls]
    den = es[0] + es[1] + es[2]
    yc = (es[0] / den) * o0_ref[...] + (es[1] / den) * o1_ref[...] + (es[2] / den) * o2_ref[...]
    merged = (_sigmoid(g0_ref[...]) * jnp.dot(ya_ref[...].astype(BF16), wpa_ref[...], preferred_element_type=F32)
              + _sigmoid(g1_ref[...]) * jnp.dot(yb_ref[...].astype(BF16), wpb_ref[...], preferred_element_type=F32)
              + _sigmoid(g2_ref[...]) * jnp.dot(yc.astype(BF16), wpc_ref[...], preferred_element_type=F32))
    z = jnp.dot(merged.astype(BF16), wo_ref[...], preferred_element_type=F32)
    out_ref[...] = x_ref[...] + _rms_rows(z, gp_ref[...])


def merge(x, ya, yb, os_, ls, proj, wpa, wpb, wpc, wo, g_post, tm):
    m = x.shape[0]
    gb = COL_GATES // D_MODEL
    row = lambda w: pl.BlockSpec((tm, w), lambda i: (i, 0))
    const = lambda a: pl.BlockSpec(a.shape, lambda i: (0, 0), pipeline_mode=pl.Buffered(1))
    in_specs = [row(D_MODEL)] + [row(HW)] * 8
    in_specs += [pl.BlockSpec((tm, D_MODEL), functools.partial(lambda i, k: (i, k), k=gb + n)) for n in range(3)]
    gp = g_post.reshape(1, D_MODEL)
    in_specs += [const(wpa), const(wpb), const(wpc), const(wo), const(gp)]
    return pl.pallas_call(
        _merge_kernel,
        grid=(m // tm,),
        in_specs=in_specs,
        out_specs=row(D_MODEL),
        out_shape=jax.ShapeDtypeStruct((m, D_MODEL), F32),
        compiler_params=_cparams("parallel"),
        name="merge",
    )(x, ya, yb, *os_, *ls, proj, proj, proj, wpa, wpb, wpc, wo, gp)


def _xattn_kernel(q_ref, mk_ref, mv_ref, o_ref):
    for h in range(HEADS):
        hs = slice(h * DH, (h + 1) * DH)
        sc = _dot_nt(q_ref[0, :, hs], mk_ref[0, 0, :, h, :]) * (DH ** -0.5)
        m = jnp.max(sc, axis=-1, keepdims=True)
        p = jnp.exp(sc - m)
        p = p / jnp.sum(p, axis=-1, keepdims=True)
        o_ref[0, :, hs] = _dot(p, mv_ref[0, 0, :, h, :])


def xattn(q3, mk5, mv5, layer, tm):
    bsz, t, _ = q3.shape
    return pl.pallas_call(
        _xattn_kernel,
        grid=(bsz, t // tm),
        in_specs=[pl.BlockSpec((1, tm, HW), lambda b, j: (b, j, 0)),
                  pl.BlockSpec((1, 1, N_MEM, HEADS, DH), lambda b, j: (layer, b, 0, 0, 0)),
                  pl.BlockSpec((1, 1, N_MEM, HEADS, DH), lambda b, j: (layer, b, 0, 0, 0))],
        out_specs=pl.BlockSpec((1, tm, HW), lambda b, j: (b, j, 0)),
        out_shape=jax.ShapeDtypeStruct((bsz, t, HW), F32),
        compiler_params=_cparams("parallel", "arbitrary"),
        name="xattn",
    )(q3, mk5, mv5)


def _proj_post_kernel(x_ref, o_ref, w_ref, g_ref, out_ref):
    z = jnp.dot(o_ref[...].astype(BF16), w_ref[...], preferred_element_type=F32)
    out_ref[...] = x_ref[...] + _rms_rows(z, g_ref[...])


def proj_post(x, o, w, g, tm):
    m = x.shape[0]
    k = o.shape[1]
    gp = g.reshape(1, D_MODEL)
    return pl.pallas_call(
        _proj_post_kernel,
        grid=(m // tm,),
        in_specs=[pl.BlockSpec((tm, D_MODEL), lambda i: (i, 0)),
                  pl.BlockSpec((tm, k), lambda i: (i, 0)),
                  pl.BlockSpec(w.shape, lambda i: (0, 0), pipeline_mode=pl.Buffered(1)),
                  pl.BlockSpec((1, D_MODEL), lambda i: (0, 0), pipeline_mode=pl.Buffered(1))],
        out_specs=pl.BlockSpec((tm, D_MODEL), lambda i: (i, 0)),
        out_shape=jax.ShapeDtypeStruct((m, D_MODEL), F32),
        compiler_params=_cparams("parallel"),
        name="proj_post",
    )(x, o, w, gp)


def _ffn_kernel(*refs, tm, use_ovr):
    if use_ovr:
        (x_ref, gpre_ref, wup_ref, cw_ref, cb_ref, wdn_ref, gpost_ref, buf_ref, ovr_ref,
         out_ref, up_ref, carry_ref, win_ref) = refs
    else:
        (x_ref, gpre_ref, wup_ref, cw_ref, cb_ref, wdn_ref, gpost_ref, buf_ref,
         out_ref, up_ref, carry_ref, win_ref) = refs
    kw = FFN_CONV - 1

    @pl.when(pl.program_id(1) == 0)
    def _():
        carry_ref[8 - kw:8, :] = buf_ref[0]

    x = x_ref[0]
    h = _rms_rows(x, gpre_ref[...]).astype(BF16)
    if use_ovr:
        slot_row = lax.broadcasted_iota(jnp.int32, (tm, 1), 0) % SAMPLE_T_PAD
        is_ovr = slot_row >= SAMPLE_T_PAD - kw
    acc = jnp.zeros((tm, D_MODEL), F32)
    for cidx in range(D_FF // FFN_COLS):
        ys = []
        for half in range(2):
            c0 = half * D_FF + cidx * FFN_COLS
            cols = slice(c0, c0 + FFN_COLS)
            u = jnp.dot(h, wup_ref[:, cols], preferred_element_type=F32)
            if use_ovr:
                u = jnp.where(is_ovr, ovr_ref[0, :, cols], u)
            up_ref[0, :, cols] = u
            win_ref[8 - kw:8, :] = carry_ref[8 - kw:8, cols]
            win_ref[8:8 + tm, :] = u
            y = u * cw_ref[kw:kw + 1, cols] + cb_ref[:, cols]
            for i in range(kw):
                y = y + win_ref[8 - kw + i:8 - kw + i + tm, :] * cw_ref[i:i + 1, cols]
            carry_ref[8 - kw:8, cols] = win_ref[8 + tm - kw:8 + tm, :]
            ys.append(y)
        y1, y2 = ys
        gelu = 0.5 * y1 * (1.0 + jnp.tanh(math.sqrt(2.0 / math.pi) * (y1 + 0.044715 * (y1 * y1 * y1))))
        a = (gelu * y2).astype(BF16)
        acc = acc + jnp.dot(a, wdn_ref[cidx * FFN_COLS:(cidx + 1) * FFN_COLS, :], preferred_element_type=F32)
    out_ref[0] = x + _rms_rows(acc, gpost_ref[...])


def conv_ffn(x3, g_pre, w_up, cw, cb, w_down, g_post, buf0, ovr3, tm):
    bsz, t, _ = x3.shape
    use_ovr = ovr3 is not None
    const = lambda a: pl.BlockSpec(a.shape, lambda b, j: (0,) * a.ndim, pipeline_mode=pl.Buffered(1))
    gpre = g_pre.reshape(1, D_MODEL)
    gpost = g_post.reshape(1, D_MODEL)
    cb2 = cb.reshape(1, 2 * D_FF)
    in_specs = [pl.BlockSpec((1, tm, D_MODEL), lambda b, j: (b, j, 0)),
                const(gpre), const(w_up), const(cw), const(cb2), const(w_down), const(gpost),
                pl.BlockSpec((1, FFN_CONV - 1, 2 * D_FF), lambda b, j: (b, 0, 0))]
    args = [x3, gpre, w_up, cw, cb2, w_down, gpost, buf0]
    if use_ovr:
        in_specs.append(pl.BlockSpec((1, tm, 2 * D_FF), lambda b, j: (b, j, 0)))
        args.append(ovr3)
    return pl.pallas_call(
        functools.partial(_ffn_kernel, tm=tm, use_ovr=use_ovr),
        grid=(bsz, t // tm),
        in_specs=in_specs,
        out_specs=[pl.BlockSpec((1, tm, D_MODEL), lambda b, j: (b, j, 0)),
                   pl.BlockSpec((1, tm, 2 * D_FF), lambda b, j: (b, j, 0))],
        out_shape=[jax.ShapeDtypeStruct((bsz, t, D_MODEL), F32),
                   jax.ShapeDtypeStruct((bsz, t, 2 * D_FF), F32)],
        scratch_shapes=[pltpu.VMEM((8, 2 * D_FF), F32), pltpu.VMEM((tm + 8, FFN_COLS), F32)],
        compiler_params=_cparams("parallel", "arbitrary"),
        name="conv_ffn",
    )(*args)


def _pick_tile(m, pref):
    t = min(m, pref)
    assert m % t == 0
    return t


def _mixer_and_ffn(x, bsz, t, t_real, W, sa, sb, sbc, caches, layer, sf, mk5, mv5, mem_layer, cos_t, sin_t):
    m = bsz * t
    prompt = caches is None
    proj = rms_matmul(x, W['g_mix_pre'], W['w_in'], _pick_tile(m, 1024), 1024)
    proj3 = proj.reshape(bsz, t, N_PROJ)
    c_rec, tb_rec = (16, 256) if prompt else (SAMPLE_T_PAD, SAMPLE_T_PAD)
    ya, sa_new = hgrn2(proj3, W['lb'], W['a_onorm'], sa, c_rec, tb_rec, t_real)
    yb, sb_new, sbc_new = gdn(proj3, W['b_conv_w'], sbc, W['alog_row'], W['dt_row'], W['b_onorm'], sb,
                              c_rec, tb_rec, t_real)
    qkv = rope_qkv(proj, cos_t, sin_t, _pick_tile(m, 512))
    if prompt:
        os_, ls = [], []
        for gi in range(C_GROUPS):
            o, lse = dil_prompt(qkv, t, gi)
            os_.append(o)
            ls.append(lse)
    else:
        os_, ls = dil_sample(qkv, bsz, caches, layer, t_real)
    rows = []
    qkv4 = qkv.reshape(QKV_PLANES, bsz, t, DH)
    nh = C_GROUPS * HEADS
    for gi in range(C_GROUPS):
        keep = min(C_WINDOWS[gi], t_real)
        for part in (1, 2):
            p0 = part * nh + gi * HEADS
            rows.append(jnp.transpose(qkv4[p0:p0 + HEADS, :, t_real - keep:t_real], (1, 2, 0, 3)))
    tm = _pick_tile(m, 256)
    x = merge(x, ya.reshape(m, HW), yb.reshape(m, HW), os_, ls, proj,
              W['w_pa'], W['w_pb'], W['w_pc'], W['w_o'], W['g_mix_post'], tm)
    q = rms_matmul(x, W['g_x_pre'], W['w_xq'], _pick_tile(m, 1024), HW)
    o = xattn(q.reshape(bsz, t, HW), mk5, mv5, mem_layer, _pick_tile(t, 256))
    x = proj_post(x, o.reshape(m, HW), W['w_xo'], W['g_x_post'], tm)
    if prompt:
        xo, up = conv_ffn(x.reshape(bsz, t, D_MODEL), W['g_ffn_pre'], W['w_up'], W['ffn_conv_w'], W['ffn_conv_b'],
                          W['w_down'], W['g_ffn_post'], sf, None, tm)
        sf_new = up[:, t_real - (FFN_CONV - 1):t_real]
    else:
        kw = FFN_CONV - 1
        nxt = jnp.concatenate([sf[1:], sf[:1]], axis=0)
        ovr = jnp.concatenate([jnp.zeros((bsz, t - kw, 2 * D_FF), F32), nxt], axis=1).reshape(1, m, 2 * D_FF)
        xo, up = conv_ffn(x.reshape(1, m, D_MODEL), W['g_ffn_pre'], W['w_up'], W['ffn_conv_w'], W['ffn_conv_b'],
                          W['w_down'], W['g_ffn_post'], sf[:1], ovr, tm)
        sf_new = up.reshape(bsz, t, 2 * D_FF)[:, t_real - kw:t_real]
    return xo.reshape(m, D_MODEL), sa_new, sb_new, sbc_new, rows, sf_new


def _rope_tables(pos):
    half = DH // 2
    inv = ROPE_THETA ** (-jnp.arange(half, dtype=F32) / half)
    ang = pos.astype(F32)[:, None] * inv[None, :]
    cos, sin = jnp.cos(ang), jnp.sin(ang)
    return jnp.concatenate([cos, cos], axis=-1), jnp.concatenate([-sin, sin], axis=-1)


def _reorder_w_in(w):
    o = 4 * HW
    a = w[:, :o]
    bqkv = w[:, o:o + B_CONV_CH]
    o += B_CONV_CH
    bz = w[:, o:o + HW]
    o += HW
    bba = w[:, o:o + 2 * HEADS]
    o += 2 * HEADS
    c = w[:, o:o + 3 * C_GROUPS * HW]
    o += 3 * C_GROUPS * HW
    gates = w[:, o:]
    pad = jnp.zeros((w.shape[0], HW - 2 * HEADS), w.dtype)
    out = jnp.concatenate([bqkv, bz, a, c, bba, pad, gates], axis=1).astype(BF16)
    assert out.shape[1] == N_PROJ
    return out


def kernel(x_prompt, x_sample, mem_prompt, state_a, state_b, state_b_conv, cache_c0_k, cache_c0_v, cache_c1_k,
           cache_c1_v, cache_c2_k, cache_c2_v, state_ffn_conv, cache_mem_k, cache_mem_v, g_mix_pre, g_mix_post,
           g_x_pre, g_x_post, g_mem, g_ffn_pre, g_ffn_post, w_in, a_lb, a_onorm, b_conv_w, b_a_log, b_dt_bias,
           b_onorm, w_pa, w_pb, w_pc, w_o, w_xq, w_xk, w_xv, w_xo, w_up, ffn_conv_w, ffn_conv_b, w_down):
    depth = w_in.shape[0]
    bp, tp, _ = x_prompt.shape
    bs, ts, _ = x_sample.shape
    tsp = SAMPLE_T_PAD
    sm = jax.nn.softmax(a_lb.astype(F32), axis=0)
    lb_all = jnp.cumsum(sm, axis=0) - sm[0]
    cos_p, sin_p = _rope_tables(jnp.arange(tp, dtype=jnp.int32))
    cos_p, sin_p = jnp.tile(cos_p, (bp, 1)), jnp.tile(sin_p, (bp, 1))
    cos_s, sin_s = _rope_tables(PAST_LEN + jnp.arange(tsp, dtype=jnp.int32))
    cos_s, sin_s = jnp.tile(cos_s, (bs, 1)), jnp.tile(sin_s, (bs, 1))
    caches = (cache_c0_k, cache_c0_v, cache_c1_k, cache_c1_v, cache_c2_k, cache_c2_v)

    xp = x_prompt.reshape(bp * tp, D_MODEL)
    xs = jnp.pad(x_sample, ((0, 0), (0, tsp - ts), (0, 0))).reshape(bs * tsp, D_MODEL)
    zeros_state = jnp.zeros((bp, HEADS, DH, DH), F32)
    zeros_bconv = jnp.zeros((bp, B_CONV - 1, B_CONV_CH), F32)
    zeros_fconv = jnp.zeros((bp, FFN_CONV - 1, 2 * D_FF), F32)
    lane8 = jnp.zeros((1, 128), F32)

    po = [[] for _ in range(12)]
    so = [[] for _ in range(10)]
    for l in range(depth):
        W = dict(g_mix_pre=g_mix_pre[l], g_mix_post=g_mix_post[l], g_x_pre=g_x_pre[l], g_x_post=g_x_post[l],
                 g_ffn_pre=g_ffn_pre[l], g_ffn_post=g_ffn_post[l], w_in=_reorder_w_in(w_in[l]), lb=lb_all[l],
                 a_onorm=a_onorm[l], b_conv_w=b_conv_w[l],
                 alog_row=lane8.at[0, HEADS:2 * HEADS].set(b_a_log[l]),
                 dt_row=lane8.at[0, HEADS:2 * HEADS].set(b_dt_bias[l]),
                 b_onorm=b_onorm[l], w_pa=w_pa[l].astype(BF16), w_pb=w_pb[l].astype(BF16),
                 w_pc=w_pc[l].astype(BF16), w_o=w_o[l].astype(BF16), w_xq=w_xq[l].astype(BF16),
                 w_xo=w_xo[l].astype(BF16), w_up=w_up[l].astype(BF16), ffn_conv_w=ffn_conv_w[l],
                 ffn_conv_b=ffn_conv_b[l], w_down=w_down[l].astype(BF16))
        w_kv = jnp.concatenate([w_xk[l], w_xv[l]], axis=1).astype(BF16)
        mkv = rms_matmul(mem_prompt.reshape(bp * N_MEM, D_MODEL), g_mem[l], w_kv, bp * N_MEM, HW)
        mkv5 = mkv.reshape(1, bp, N_MEM, 2 * HEADS, DH)
        mk = mkv5[0, :, :, :HEADS]
        mv = mkv5[0, :, :, HEADS:]
        xp, sa, sb, sbc, rows, sf = _mixer_and_ffn(
            xp, bp, tp, tp, W, zeros_state, zeros_state, zeros_bconv, None, l, zeros_fconv,
            mk[None], mv[None], 0, cos_p, sin_p)
        for i, a in enumerate([sa, sb, sbc] + rows + [sf, mk, mv]):
            po[i].append(a)
        xs, sa, sb, sbc, rows, sf = _mixer_and_ffn(
            xs, bs, tsp, ts, W, state_a[l], state_b[l], state_b_conv[l], caches, l, state_ffn_conv[l],
            cache_mem_k, cache_mem_v, l, cos_s, sin_s)
        for i, a in enumerate([sa, sb, sbc] + rows + [sf]):
            so[i].append(a)
    p_out = [jnp.stack(a, axis=0) for a in po]
    s_out = [jnp.stack(a, axis=0) for a in so]
    y_prompt = xp.reshape(bp, tp, D_MODEL)
    y_sample = xs.reshape(bs, tsp, D_MODEL)[:, :ts]
    return tuple([y_prompt, y_sample] + p_out + s_out)
```

```python
import functools
import math

import jax
import jax.numpy as jnp
from jax import lax
from jax.experimental import pallas as pl
from jax.experimental.pallas import tpu as pltpu

F32 = jnp.float32
BF16 = jnp.bfloat16

D_MODEL = 1024
PAST_LEN = 2048
EPS = 1e-6
NEG = -1e30
LB_FLOOR = 1e-30
ROPE_THETA = 10000.0
HEADS = 4
DH = 128
HW = HEADS * DH
B_CONV = 4
B_CONV_CH = 3 * HW
C_WINDOWS = (128, 512, 2048)
C_DILATIONS = (1, 4, 16)
C_GROUPS = 3
C_BAND = 128
C_BLK = 128
N_MEM = 256
D_FF = 128 * ((8 * D_MODEL // 3 + 127) // 128)
FFN_CONV = 3
FFN_COLS = 256
SAMPLE_T_PAD = 8
GDN_SAMPLE_SEQS = 4

COL_BQKV = 0
COL_BZ = 1536
COL_A = 2048
COL_CQ = 4096
COL_CK = 5632
COL_CV = 7168
COL_BBA = 8704
COL_GATES = 9216
N_PROJ = 12288

VMEM_LIMIT_BYTES = 56 * 1024 * 1024


def _cparams(*sem):
    return pltpu.CompilerParams(dimension_semantics=sem, vmem_limit_bytes=VMEM_LIMIT_BYTES)


def _dot(a, b):
    return jnp.dot(a.astype(BF16), b.astype(BF16), preferred_element_type=F32)


def _dot_nt(a, b):
    return lax.dot_general(a.astype(BF16), b.astype(BF16), (((1,), (1,)), ((), ())), preferred_element_type=F32)


def _dot_tn(a, b):
    return lax.dot_general(a.astype(BF16), b.astype(BF16), (((0,), (0,)), ((), ())), preferred_element_type=F32)


def _cumsum_rows(tri, x):
    return jnp.dot(tri, x, precision=lax.Precision.HIGHEST, preferred_element_type=F32)


def _rms_rows(x, g):
    return x * lax.rsqrt(jnp.mean(x * x, axis=-1, keepdims=True) + EPS) * g


def _sigmoid(x):
    return jax.nn.sigmoid(x)


def _silu(x):
    return x * jax.nn.sigmoid(x)


def _log1p_exp_neg_abs(x):
    return jnp.log1p(jnp.exp(-jnp.abs(x)))


def _tri(c):
    r = lax.broadcasted_iota(jnp.int32, (c, c), 0)
    s = lax.broadcasted_iota(jnp.int32, (c, c), 1)
    return r, s


def _rms_matmul_kernel(x_ref, g_ref, w_ref, o_ref, xn_ref):
    @pl.when(pl.program_id(1) == 0)
    def _():
        xn_ref[...] = _rms_rows(x_ref[...], g_ref[...]).astype(BF16)

    o_ref[...] = jnp.dot(xn_ref[...], w_ref[...], preferred_element_type=F32)


def rms_matmul(x, g, w, tm, tn):
    m, k = x.shape
    n = w.shape[1]
    return pl.pallas_call(
        _rms_matmul_kernel,
        grid=(m // tm, n // tn),
        in_specs=[pl.BlockSpec((tm, k), lambda i, j: (i, 0)),
                  pl.BlockSpec((1, k), lambda i, j: (0, 0)),
                  pl.BlockSpec((k, tn), lambda i, j: (0, j))],
        out_specs=pl.BlockSpec((tm, tn), lambda i, j: (i, j)),
        out_shape=jax.ShapeDtypeStruct((m, n), F32),
        scratch_shapes=[pltpu.VMEM((tm, k), BF16)],
        compiler_params=_cparams("parallel", "arbitrary"),
        name="rms_matmul",
    )(x, g.reshape(1, k), w)


def _hgrn2_kernel(pq_ref, pf_ref, pi_ref, pg_ref, lb_ref, on_ref, s0_ref, y_ref, s_ref, *, c, tb, t_real):
    j = pl.program_id(1)
    n_j = pl.num_programs(1)

    @pl.when(j == 0)
    def _():
        for h in range(HEADS):
            s_ref[0, h] = s0_ref[0, 0, h].T

    lb = lb_ref[...]
    log_lb = jnp.log(jnp.maximum(lb, LB_FLOOR))
    log_1mlb = jnp.log1p(-lb)
    onorm = on_ref[...]
    r_i, s_i = _tri(c)
    tri = (r_i >= s_i).astype(F32)
    row = lax.broadcasted_iota(jnp.int32, (c, 1), 0)
    n_diag = min(c, t_real)

    def chunk(i, carry):
        r0 = pl.multiple_of(i * c, c)
        rows = pl.ds(r0, c)
        zq = pq_ref[0, rows, :]
        zf = pf_ref[0, rows, :]
        v = pi_ref[0, rows, :]
        zg = pg_ref[0, rows, :]
        q = _silu(zq) * (DH ** -0.5)
        log_sig = jnp.minimum(zf, 0.0) - _log1p_exp_neg_abs(zf)
        t1 = log_1mlb + log_sig
        logf = jnp.maximum(log_lb, t1) + _log1p_exp_neg_abs(log_lb - t1)
        k = -jnp.tanh(0.5 * logf) * (jnp.exp(logf) + 1.0)
        if t_real < tb:
            live = (j * tb + r0 + row) < t_real
            logf = jnp.where(live, logf, 0.0)
            k = jnp.where(live, k, 0.0)
        b = _cumsum_rows(tri, logf)
        bl = b[c - 1:c, :]
        qs = q * jnp.exp(b)
        kd = k * jnp.exp(bl - b)
        el = jnp.exp(bl)
        for h in range(HEADS):
            hs = slice(h * DH, (h + 1) * DH)
            st = s_ref[0, h]
            o = _dot_nt(qs[:, hs], st)
            bh, qh, kh, vh = b[:, hs], q[:, hs], k[:, hs], v[:, hs]
            for r8 in range(0, n_diag, 8):
                o_part = o[r8:]
                for s in range(r8, min(r8 + 8, n_diag)):
                    m = row[r8:] >= s
                    d = jnp.where(m, bh[r8:] - bh[s:s + 1, :], 0.0)
                    a = qh[r8:] * kh[s:s + 1, :] * jnp.exp(d)
                    col = jnp.where(m, jnp.sum(a, axis=-1, keepdims=True), 0.0)
                    o_part = o_part + col * vh[s:s + 1, :]
                o = o_part if r8 == 0 else jnp.concatenate([o[:r8], o_part], axis=0)
            s_ref[0, h] = st * el[:, hs] + _dot_tn(vh, kd[:, hs])
            y_ref[0, rows, hs] = _rms_rows(o, onorm) * _silu(zg[:, hs])
        return carry

    lax.fori_loop(0, tb // c, chunk, 0)

    @pl.when(j == n_j - 1)
    def _():
        for h in range(HEADS):
            s_ref[0, h] = s_ref[0, h].T


def hgrn2(proj3, lb, onorm, s0_all, layer, c, tb, t_real):
    bsz, t, _ = proj3.shape
    cb = COL_A // HW
    in_specs = [pl.BlockSpec((1, tb, HW), functools.partial(lambda b, j, k: (b, j, k), k=cb + i)) for i in range(4)]
    in_specs += [pl.BlockSpec((1, HW), lambda b, j: (0, 0)),
                 pl.BlockSpec((1, DH), lambda b, j: (0, 0)),
                 pl.BlockSpec((1, 1, HEADS, DH, DH), lambda b, j: (layer, b, 0, 0, 0))]
    return pl.pallas_call(
        functools.partial(_hgrn2_kernel, c=c, tb=tb, t_real=t_real),
        grid=(bsz, t // tb),
        in_specs=in_specs,
        out_specs=[pl.BlockSpec((1, tb, HW), lambda b, j: (b, j, 0)),
                   pl.BlockSpec((1, HEADS, DH, DH), lambda b, j: (b, 0, 0, 0))],
        out_shape=[jax.ShapeDtypeStruct((bsz, t, HW), F32),
                   jax.ShapeDtypeStruct((bsz, HEADS, DH, DH), F32)],
        compiler_params=_cparams("parallel", "arbitrary"),
        name="hgrn2",
    )(proj3, proj3, proj3, proj3, lb.reshape(1, HW), onorm.reshape(1, DH), s0_all)


def _gdn_kernel(x_ref, z_ref, ba_ref, cw_ref, buf_ref, alog_ref, dt_ref, on_ref, s0_ref,
                y_ref, s_ref, nbuf_ref, win_ref, act_ref, *, c, tb, t_real, nb):
    j = pl.program_id(1)
    n_j = pl.num_programs(1)
    kw = B_CONV - 1

    @pl.when(j == 0)
    def _():
        for n in range(nb):
            win_ref[n, 8 - kw:8, :] = buf_ref[0, n]
        s_ref[...] = s0_ref[0]

    for n in range(nb):
        win_ref[n, 8:8 + tb, :] = x_ref[n]
        conv = win_ref[n, 8 - kw:8 - kw + tb, :] * cw_ref[0:1, :]
        for i in range(1, B_CONV):
            conv = conv + win_ref[n, 8 - kw + i:8 - kw + i + tb, :] * cw_ref[i:i + 1, :]
        act_ref[n] = _silu(conv)

    @pl.when(j == n_j - 1)
    def _():
        last = t_real - (t_real - 1) // tb * tb
        for n in range(nb):
            nbuf_ref[n] = win_ref[n, 8 + last - kw:8 + last, :]

    for n in range(nb):
        win_ref[n, 8 - kw:8, :] = win_ref[n, 8 + tb - kw:8 + tb, :]

    onorm = on_ref[...]
    r_i, s_i = _tri(c)
    tri = (r_i >= s_i).astype(F32)
    strict = r_i > s_i
    incl = r_i >= s_i
    row = lax.broadcasted_iota(jnp.int32, (c, 1), 0)

    def chunk(i, carry):
        r0 = pl.multiple_of(i * c, c)
        rows = pl.ds(r0, c)
        qss, qks, kds, bls, lows, rhs = [], [], [], [], [], []
        for n in range(nb):
            blk = ba_ref[n, rows, :]
            beta_all = _sigmoid(blk)
            sp_in = blk + dt_ref[...]
            softplus = jnp.maximum(sp_in, 0.0) + _log1p_exp_neg_abs(sp_in)
            g_all = -jnp.exp(alog_ref[...]) * softplus
            if t_real < tb:
                live = (j * tb + r0 + row) < t_real
                beta_all = jnp.where(live, beta_all, 0.0)
                g_all = jnp.where(live, g_all, 0.0)
            bg = _cumsum_rows(tri, g_all)
            bg_t = bg.T
            for h in range(HEADS):
                qa = act_ref[n, rows, h * DH:(h + 1) * DH]
                ka = act_ref[n, rows, HW + h * DH:HW + (h + 1) * DH]
                vh = act_ref[n, rows, 2 * HW + h * DH:2 * HW + (h + 1) * DH]
                qh = qa * lax.rsqrt(jnp.sum(qa * qa, axis=-1, keepdims=True) + EPS) * (DH ** -0.5)
                kh = ka * lax.rsqrt(jnp.sum(ka * ka, axis=-1, keepdims=True) + EPS)
                beta = beta_all[:, h:h + 1]
                b_col = bg[:, HEADS + h:HEADS + h + 1]
                b_row = bg_t[HEADS + h:HEADS + h + 1, :]
                diff = b_col - b_row
                dec_s = jnp.where(strict, jnp.exp(jnp.where(strict, diff, 0.0)), 0.0)
                dec_i = jnp.where(incl, jnp.exp(jnp.where(incl, diff, 0.0)), 0.0)
                st = s_ref[n, h]
                eb = jnp.exp(b_col)
                bl = b_col[c - 1:c, :]
                lows.append(beta * _dot_nt(kh, kh) * dec_s)
                rhs.append(beta * (vh - eb * _dot(kh, st)))
                qss.append(eb * _dot(qh, st))
                qks.append(_dot_nt(qh, kh) * dec_i)
                kds.append(kh * jnp.exp(bl - b_col))
                bls.append(bl)
        u = jnp.concatenate(rhs, axis=1)
        for t in range(c - 1):
            lcol = jnp.concatenate([jnp.broadcast_to(lw[:, t:t + 1], (c, DH)) for lw in lows], axis=1)
            u = u - lcol * u[t:t + 1, :]
        for n in range(nb):
            z = z_ref[n, rows, :]
            for h in range(HEADS):
                e = n * HEADS + h
                hs = slice(h * DH, (h + 1) * DH)
                uh = u[:, e * DH:(e + 1) * DH]
                o = qss[e] + _dot(qks[e], uh)
                s_ref[n, h] = s_ref[n, h] * jnp.exp(bls[e]) + _dot_tn(kds[e], uh)
                y_ref[n, rows, hs] = _rms_rows(o, onorm) * _silu(z[:, hs])
        return carry

    lax.fori_loop(0, tb // c, chunk, 0)


def gdn(proj3, conv_w, conv_buf_all, alog_row, dt_row, onorm, s0_all, layer, c, tb, t_real, nb):
    bsz, t, _ = proj3.shape
    assert t_real - (t_real - 1) // tb * tb >= B_CONV - 1 and bsz % nb == 0
    return pl.pallas_call(
        functools.partial(_gdn_kernel, c=c, tb=tb, t_real=t_real, nb=nb),
        grid=(bsz // nb, t // tb),
        in_specs=[pl.BlockSpec((nb, tb, B_CONV_CH), lambda b, j: (b, j, COL_BQKV // B_CONV_CH)),
                  pl.BlockSpec((nb, tb, HW), lambda b, j: (b, j, COL_BZ // HW)),
                  pl.BlockSpec((nb, tb, 128), lambda b, j: (b, j, COL_BBA // 128)),
                  pl.BlockSpec((B_CONV, B_CONV_CH), lambda b, j: (0, 0)),
                  pl.BlockSpec((1, nb, B_CONV - 1, B_CONV_CH), lambda b, j: (layer, b, 0, 0)),
                  pl.BlockSpec((1, 128), lambda b, j: (0, 0)),
                  pl.BlockSpec((1, 128), lambda b, j: (0, 0)),
                  pl.BlockSpec((1, DH), lambda b, j: (0, 0)),
                  pl.BlockSpec((1, nb, HEADS, DH, DH), lambda b, j: (layer, b, 0, 0, 0))],
        out_specs=[pl.BlockSpec((nb, tb, HW), lambda b, j: (b, j, 0)),
                   pl.BlockSpec((nb, HEADS, DH, DH), lambda b, j: (b, 0, 0, 0)),
                   pl.BlockSpec((nb, B_CONV - 1, B_CONV_CH), lambda b, j: (b, 0, 0))],
        out_shape=[jax.ShapeDtypeStruct((bsz, t, HW), F32),
                   jax.ShapeDtypeStruct((bsz, HEADS, DH, DH), F32),
                   jax.ShapeDtypeStruct((bsz, B_CONV - 1, B_CONV_CH), F32)],
        scratch_shapes=[pltpu.VMEM((nb, tb + 8, B_CONV_CH), F32), pltpu.VMEM((nb, tb, B_CONV_CH), F32)],
        compiler_params=_cparams("parallel", "arbitrary"),
        name="gdn",
    )(proj3, proj3, proj3, conv_w, conv_buf_all, alog_row, dt_row, onorm.reshape(1, DH), s0_all)


QKV_PLANES = 3 * C_GROUPS * HEADS


def _rope_kernel(x_ref, cos_ref, sin_ref, o_ref):
    j = pl.program_id(1)

    @pl.when(j < 2 * C_GROUPS)
    def _():
        cs = cos_ref[...]
        sn = sin_ref[...]
        for h in range(HEADS):
            x = x_ref[:, h * DH:(h + 1) * DH]
            o_ref[h] = x * cs + pltpu.roll(x, DH // 2, 1) * sn

    @pl.when(j >= 2 * C_GROUPS)
    def _():
        for h in range(HEADS):
            o_ref[h] = x_ref[:, h * DH:(h + 1) * DH]


def rope_qkv(proj, cos_t, sin_t, tm):
    m = proj.shape[0]
    cb = COL_CQ // HW
    return pl.pallas_call(
        _rope_kernel,
        grid=(m // tm, 3 * C_GROUPS),
        in_specs=[pl.BlockSpec((tm, HW), lambda i, j: (i, cb + j)),
                  pl.BlockSpec((tm, DH), lambda i, j: (i, 0)),
                  pl.BlockSpec((tm, DH), lambda i, j: (i, 0))],
        out_specs=pl.BlockSpec((HEADS, tm, DH), lambda i, j: (j, i, 0)),
        out_shape=jax.ShapeDtypeStruct((QKV_PLANES, m, DH), F32),
        compiler_params=_cparams("parallel", "arbitrary"),
        name="rope_qkv",
    )(proj, cos_t, sin_t)


DIL_ROWS = C_BLK * max(C_DILATIONS)


def _dil_prompt_kernel(q_ref, kc_ref, vc_ref, kp_ref, vp_ref, o_ref, lse_ref, *, d, t_blocks):
    first = (pl.program_id(0) % t_blocks) == 0
    span = C_BLK * d
    qi, ki = _tri(C_BLK)
    in_band = ki >= qi
    mask_cur = ki <= qi
    for blk in range(DIL_ROWS // span):
        for r in range(d):
            rows = pl.ds(blk * span + r, C_BLK, stride=d)
            q = q_ref[0, rows, :]
            if blk > 0:
                prows = pl.ds((blk - 1) * span + r, C_BLK, stride=d)
                k_prev, v_prev, mask_prev = kc_ref[0, prows, :], vc_ref[0, prows, :], in_band
            else:
                prows = pl.ds(r, C_BLK, stride=d)
                k_prev, v_prev = kp_ref[0, prows, :], vp_ref[0, prows, :]
                mask_prev = in_band & jnp.logical_not(first)
            sp = jnp.where(mask_prev, _dot_nt(q, k_prev) * (DH ** -0.5), NEG)
            sc = jnp.where(mask_cur, _dot_nt(q, kc_ref[0, rows, :]) * (DH ** -0.5), NEG)
            m = jnp.maximum(jnp.max(sp, axis=-1, keepdims=True), jnp.max(sc, axis=-1, keepdims=True))
            pp = jnp.exp(sp - m)
            pc = jnp.exp(sc - m)
            l = jnp.sum(pp, axis=-1, keepdims=True) + jnp.sum(pc, axis=-1, keepdims=True)
            o_ref[rows, :] = (_dot(pp, v_prev) + _dot(pc, vc_ref[0, rows, :])) / l
            lse_ref[rows, :] = jnp.broadcast_to(m + jnp.log(l), (C_BLK, DH))


def dil_prompt(qkv, t, gi):
    m = qkv.shape[1]
    d = C_DILATIONS[gi]
    span = C_BLK * d
    r_blk = DIL_ROWS
    assert t % r_blk == 0 and r_blk % span == 0
    per = r_blk // span
    nh = C_GROUPS * HEADS

    def cur(part):
        return pl.BlockSpec((1, r_blk, DH), lambda i, h: (part * nh + gi * HEADS + h, i, 0))

    def prev(part):
        return pl.BlockSpec((1, span, DH), lambda i, h: (part * nh + gi * HEADS + h, jnp.maximum(i * per - 1, 0), 0))

    return pl.pallas_call(
        functools.partial(_dil_prompt_kernel, d=d, t_blocks=t // r_blk),
        grid=(m // r_blk, HEADS),
        in_specs=[cur(0), cur(1), cur(2), prev(1), prev(2)],
        out_specs=[pl.BlockSpec((r_blk, DH), lambda i, h: (i, h))] * 2,
        out_shape=[jax.ShapeDtypeStruct((m, HW), F32)] * 2,
        compiler_params=_cparams("parallel", "arbitrary"),
        name="dil_prompt_g%d" % gi,
    )(qkv, qkv, qkv, qkv, qkv)


SAMPLE_QKV_ROWS = SAMPLE_T_PAD * HEADS


def _store_rows_th(o_ref, l_ref, o, lse, t, rows0):
    for h in range(HEADS):
        hs = slice(h * DH, (h + 1) * DH)
        o_ref[0, t:t + 1, hs] = o[rows0 + h:rows0 + h + 1, :]
        l_ref[0, t:t + 1, hs] = jnp.broadcast_to(lse[rows0 + h:rows0 + h + 1, :], (1, DH))


def _dil_sample_kernel(qkv_ref, k0c, v0c, k1c, v1c, k2c, v2c,
                       o0_ref, o1_ref, o2_ref, l0_ref, l1_ref, l2_ref, *, t_real):
    kc = (k0c, k1c, k2c)
    vc = (v0c, v1c, v2c)
    o_refs = (o0_ref, o1_ref, o2_ref)
    l_refs = (l0_ref, l1_ref, l2_ref)
    scale = DH ** -0.5
    nq = t_real * HEADS
    for gi in range(C_GROUPS):
        o_refs[gi][...] = jnp.zeros_like(o_refs[gi])
        l_refs[gi][...] = jnp.zeros_like(l_refs[gi])
    for gi in range(1, C_GROUPS):
        qm = qkv_ref[0, gi, 0:nq, :]
        kn = qkv_ref[0, C_GROUPS + gi, 0:nq, :]
        vn = qkv_ref[0, 2 * C_GROUPS + gi, 0:nq, :]
        kb = kc[gi][0, 0]
        vb = vc[gi][0, 0]
        s = jnp.sum(kb * qm[None], axis=-1, keepdims=True) * scale
        sn = jnp.sum(kn * qm, axis=-1, keepdims=True) * scale
        m = jnp.maximum(jnp.max(s, axis=0), sn)
        p = jnp.exp(s - m[None])
        pn = jnp.exp(sn - m)
        l = jnp.sum(p, axis=0) + pn
        o = (jnp.sum(p * vb, axis=0) + pn * vn) / l
        lse = m + jnp.log(l)
        for t in range(t_real):
            _store_rows_th(o_refs[gi], l_refs[gi], o, lse, t, t * HEADS)
    half = lax.broadcasted_iota(jnp.int32, (2 * HEADS, 1), 0) >= HEADS
    n_pair = C_BAND // 2
    pos = 2 * lax.broadcasted_iota(jnp.int32, (n_pair, 2 * HEADS, 1), 0) + half[None].astype(jnp.int32)
    t_new = 2 * lax.broadcasted_iota(jnp.int32, (nq // (2 * HEADS), 2 * HEADS, 1), 0) + half[None].astype(jnp.int32)
    qm = qkv_ref[0, 0, 0:nq, :]
    kn = qkv_ref[0, C_GROUPS, 0:nq, :].reshape(nq // (2 * HEADS), 2 * HEADS, DH)
    vn = qkv_ref[0, 2 * C_GROUPS, 0:nq, :].reshape(nq // (2 * HEADS), 2 * HEADS, DH)
    k0 = kc[0][0, 0]
    v0 = vc[0][0, 0]
    for t in range(t_real):
        pair = qm[2 * HEADS * (t // 2):2 * HEADS * (t // 2 + 1), :]
        swapped = pltpu.roll(pair, HEADS, 0)
        q8 = jnp.where(half == (t % 2 == 1), pair, swapped)
        s = jnp.sum(k0 * q8[None], axis=-1, keepdims=True) * scale
        s = jnp.where(pos >= t, s, NEG)
        sn = jnp.sum(kn * q8[None], axis=-1, keepdims=True) * scale
        sn = jnp.where(t_new <= t, sn, NEG)
        m8 = jnp.maximum(jnp.max(s, axis=0), jnp.max(sn, axis=0))
        m = jnp.maximum(m8, pltpu.roll(m8, HEADS, 0))
        p = jnp.exp(s - m[None])
        pn = jnp.exp(sn - m[None])
        l8 = jnp.sum(p, axis=0) + jnp.sum(pn, axis=0)
        l = l8 + pltpu.roll(l8, HEADS, 0)
        o8 = jnp.sum(p * v0, axis=0) + jnp.sum(pn * vn, axis=0)
        o = (o8 + pltpu.roll(o8, HEADS, 0)) / l
        _store_rows_th(o_refs[0], l_refs[0], o, m + jnp.log(l), t, 0)


def dil_sample(qkv, bsz, caches, layer, t_real):
    tp = SAMPLE_T_PAD
    assert t_real % 2 == 0
    qkv_th = qkv.reshape(3 * C_GROUPS, HEADS, bsz, tp, DH).transpose(2, 0, 3, 1, 4)
    qkv_th = qkv_th.reshape(bsz, 3 * C_GROUPS, SAMPLE_QKV_ROWS, DH)
    views = []
    specs = []
    for gi in range(C_GROUPS):
        d = C_DILATIONS[gi]
        for cch in (caches[2 * gi], caches[2 * gi + 1]):
            depth, db, ln = cch.shape[:3]
            assert ln == C_WINDOWS[gi] and ln // d == C_BAND and (gi == 0 or t_real <= d)
            if gi == 0:
                views.append(cch.reshape(depth, db, C_BAND // 2, 2 * HEADS, DH))
                specs.append(pl.BlockSpec((1, 1, C_BAND // 2, 2 * HEADS, DH),
                                          functools.partial(lambda b, l: (l, b, 0, 0, 0), l=layer)))
            else:
                views.append(cch.reshape(depth, db, C_BAND, d * HEADS, DH))
                specs.append(pl.BlockSpec((1, 1, C_BAND, t_real * HEADS, DH),
                                          functools.partial(lambda b, l: (l, b, 0, 0, 0), l=layer)))
    outs = pl.pallas_call(
        functools.partial(_dil_sample_kernel, t_real=t_real),
        grid=(bsz,),
        in_specs=[pl.BlockSpec((1, 3 * C_GROUPS, SAMPLE_QKV_ROWS, DH), lambda b: (b, 0, 0, 0))] + specs,
        out_specs=[pl.BlockSpec((1, tp, HW), lambda b: (b, 0, 0))] * (2 * C_GROUPS),
        out_shape=[jax.ShapeDtypeStruct((bsz, tp, HW), F32)] * (2 * C_GROUPS),
        compiler_params=_cparams("parallel"),
        name="dil_sample",
    )(qkv_th, *views)
    o = [a.reshape(bsz * tp, HW) for a in outs[:C_GROUPS]]
    lse = [a.reshape(bsz * tp, HW) for a in outs[C_GROUPS:]]
    return o, lse


def _merge_kernel(x_ref, ya_ref, yb_ref, o0_ref, o1_ref, o2_ref, l0_ref, l1_ref, l2_ref,
                  g0_ref, g1_ref, g2_ref, wpa_ref, wpb_ref, wpc_ref, wo_ref, gp_ref, out_ref):
    ls = (l0_ref[...], l1_ref[...], l2_ref[...])
    m = jnp.maximum(jnp.maximum(ls[0], ls[1]), ls[2])
    es = [jnp.exp(a - m) for a in ls]
    den = es[0] + es[1] + es[2]
    yc = (es[0] / den) * o0_ref[...] + (es[1] / den) * o1_ref[...] + (es[2] / den) * o2_ref[...]
    merged = (_sigmoid(g0_ref[...]) * jnp.dot(ya_ref[...].astype(BF16), wpa_ref[...], preferred_element_type=F32)
              + _sigmoid(g1_ref[...]) * jnp.dot(yb_ref[...].astype(BF16), wpb_ref[...], preferred_element_type=F32)
              + _sigmoid(g2_ref[...]) * jnp.dot(yc.astype(BF16), wpc_ref[...], preferred_element_type=F32))
    z = jnp.dot(merged.astype(BF16), wo_ref[...], preferred_element_type=F32)
    out_ref[...] = x_ref[...] + _rms_rows(z, gp_ref[...])


def merge(x, ya, yb, os_, ls, proj, wpa, wpb, wpc, wo, g_post, tm):
    m = x.shape[0]
    gb = COL_GATES // D_MODEL
    row = lambda w: pl.BlockSpec((tm, w), lambda i: (i, 0))
    const = lambda a: pl.BlockSpec(a.shape, lambda i: (0, 0), pipeline_mode=pl.Buffered(1))
    in_specs = [row(D_MODEL)] + [row(HW)] * 8
    in_specs += [pl.BlockSpec((tm, D_MODEL), functools.partial(lambda i, k: (i, k), k=gb + n)) for n in range(3)]
    gp = g_post.reshape(1, D_MODEL)
    in_specs += [const(wpa), const(wpb), const(wpc), const(wo), const(gp)]
    return pl.pallas_call(
        _merge_kernel,
        grid=(m // tm,),
        in_specs=in_specs,
        out_specs=row(D_MODEL),
        out_shape=jax.ShapeDtypeStruct((m, D_MODEL), F32),
        compiler_params=_cparams("parallel"),
        name="merge",
    )(x, ya, yb, *os_, *ls, proj, proj, proj, wpa, wpb, wpc, wo, gp)


def _xattn_kernel(q_ref, mk_ref, mv_ref, o_ref):
    for h in range(HEADS):
        hs = slice(h * DH, (h + 1) * DH)
        sc = _dot_nt(q_ref[0, :, hs], mk_ref[0, 0, :, h, :]) * (DH ** -0.5)
        m = jnp.max(sc, axis=-1, keepdims=True)
        p = jnp.exp(sc - m)
        p = p / jnp.sum(p, axis=-1, keepdims=True)
        o_ref[0, :, hs] = _dot(p, mv_ref[0, 0, :, h, :])


def xattn(q3, mk5, mv5, layer, tm):
    bsz, t, _ = q3.shape
    return pl.pallas_call(
        _xattn_kernel,
        grid=(bsz, t // tm),
        in_specs=[pl.BlockSpec((1, tm, HW), lambda b, j: (b, j, 0)),
                  pl.BlockSpec((1, 1, N_MEM, HEADS, DH), lambda b, j: (layer, b, 0, 0, 0)),
                  pl.BlockSpec((1, 1, N_MEM, HEADS, DH), lambda b, j: (layer, b, 0, 0, 0))],
        out_specs=pl.BlockSpec((1, tm, HW), lambda b, j: (b, j, 0)),
        out_shape=jax.ShapeDtypeStruct((bsz, t, HW), F32),
        compiler_params=_cparams("parallel", "arbitrary"),
        name="xattn",
    )(q3, mk5, mv5)


def _proj_post_kernel(x_ref, o_ref, w_ref, g_ref, out_ref):
    z = jnp.dot(o_ref[...].astype(BF16), w_ref[...], preferred_element_type=F32)
    out_ref[...] = x_ref[...] + _rms_rows(z, g_ref[...])


def proj_post(x, o, w, g, tm):
    m = x.shape[0]
    k = o.shape[1]
    gp = g.reshape(1, D_MODEL)
    return pl.pallas_call(
        _proj_post_kernel,
        grid=(m // tm,),
        in_specs=[pl.BlockSpec((tm, D_MODEL), lambda i: (i, 0)),
                  pl.BlockSpec((tm, k), lambda i: (i, 0)),
                  pl.BlockSpec(w.shape, lambda i: (0, 0), pipeline_mode=pl.Buffered(1)),
                  pl.BlockSpec((1, D_MODEL), lambda i: (0, 0), pipeline_mode=pl.Buffered(1))],
        out_specs=pl.BlockSpec((tm, D_MODEL), lambda i: (i, 0)),
        out_shape=jax.ShapeDtypeStruct((m, D_MODEL), F32),
        compiler_params=_cparams("parallel"),
        name="proj_post",
    )(x, o, w, gp)


def _ffn_kernel(*refs, tm, use_ovr):
    if use_ovr:
        (x_ref, gpre_ref, wup_ref, cw_ref, cb_ref, wdn_ref, gpost_ref, buf_ref, ovr_ref,
         out_ref, up_ref, carry_ref, win_ref) = refs
    else:
        (x_ref, gpre_ref, wup_ref, cw_ref, cb_ref, wdn_ref, gpost_ref, buf_ref,
         out_ref, up_ref, carry_ref, win_ref) = refs
    kw = FFN_CONV - 1

    @pl.when(pl.program_id(1) == 0)
    def _():
        carry_ref[8 - kw:8, :] = buf_ref[0]

    x = x_ref[0]
    h = _rms_rows(x, gpre_ref[...]).astype(BF16)
    if use_ovr:
        slot_row = lax.broadcasted_iota(jnp.int32, (tm, 1), 0) % SAMPLE_T_PAD
        is_ovr = slot_row >= SAMPLE_T_PAD - kw
    acc = jnp.zeros((tm, D_MODEL), F32)
    for cidx in range(D_FF // FFN_COLS):
        ys = []
        for half in range(2):
            c0 = half * D_FF + cidx * FFN_COLS
            cols = slice(c0, c0 + FFN_COLS)
            u = jnp.dot(h, wup_ref[:, cols], preferred_element_type=F32)
            if use_ovr:
                u = jnp.where(is_ovr, ovr_ref[0, :, cols], u)
            up_ref[0, :, cols] = u if use_ovr else u[tm - 8:tm]
            win_ref[8 - kw:8, :] = carry_ref[8 - kw:8, cols]
            win_ref[8:8 + tm, :] = u
            y = u * cw_ref[kw:kw + 1, cols] + cb_ref[:, cols]
            for i in range(kw):
                y = y + win_ref[8 - kw + i:8 - kw + i + tm, :] * cw_ref[i:i + 1, cols]
            carry_ref[8 - kw:8, cols] = win_ref[8 + tm - kw:8 + tm, :]
            ys.append(y)
        y1, y2 = ys
        gelu = 0.5 * y1 * (1.0 + jnp.tanh(math.sqrt(2.0 / math.pi) * (y1 + 0.044715 * (y1 * y1 * y1))))
        a = (gelu * y2).astype(BF16)
        acc = acc + jnp.dot(a, wdn_ref[cidx * FFN_COLS:(cidx + 1) * FFN_COLS, :], preferred_element_type=F32)
    out_ref[0] = x + _rms_rows(acc, gpost_ref[...])


def conv_ffn(x3, g_pre, w_up, cw, cb, w_down, g_post, buf0, ovr3, tm):
    bsz, t, _ = x3.shape
    use_ovr = ovr3 is not None
    const = lambda a: pl.BlockSpec(a.shape, lambda b, j: (0,) * a.ndim, pipeline_mode=pl.Buffered(1))
    gpre = g_pre.reshape(1, D_MODEL)
    gpost = g_post.reshape(1, D_MODEL)
    cb2 = cb.reshape(1, 2 * D_FF)
    in_specs = [pl.BlockSpec((1, tm, D_MODEL), lambda b, j: (b, j, 0)),
                const(gpre), const(w_up), const(cw), const(cb2), const(w_down), const(gpost),
                pl.BlockSpec((1, FFN_CONV - 1, 2 * D_FF), lambda b, j: (b, 0, 0))]
    args = [x3, gpre, w_up, cw, cb2, w_down, gpost, buf0]
    if use_ovr:
        in_specs.append(pl.BlockSpec((1, tm, 2 * D_FF), lambda b, j: (b, j, 0)))
        args.append(ovr3)
    return pl.pallas_call(
        functools.partial(_ffn_kernel, tm=tm, use_ovr=use_ovr),
        grid=(bsz, t // tm),
        in_specs=in_specs,
        out_specs=[pl.BlockSpec((1, tm, D_MODEL), lambda b, j: (b, j, 0)),
                   pl.BlockSpec((1, tm, 2 * D_FF), lambda b, j: (b, j, 0)) if use_ovr
                   else pl.BlockSpec((1, 8, 2 * D_FF), lambda b, j: (b, 0, 0))],
        out_shape=[jax.ShapeDtypeStruct((bsz, t, D_MODEL), F32),
                   jax.ShapeDtypeStruct((bsz, t if use_ovr else 8, 2 * D_FF), F32)],
        scratch_shapes=[pltpu.VMEM((8, 2 * D_FF), F32), pltpu.VMEM((tm + 8, FFN_COLS), F32)],
        compiler_params=_cparams("parallel", "arbitrary"),
        name="conv_ffn",
    )(*args)


def _pick_tile(m, pref):
    t = min(m, pref)
    assert m % t == 0
    return t


def _mixer_and_ffn(x, bsz, t, t_real, W, sa, sb, sbc, caches, layer, sf, mk5, mv5, mem_layer, cos_t, sin_t):
    m = bsz * t
    prompt = caches is None
    proj = rms_matmul(x, W['g_mix_pre'], W['w_in'], _pick_tile(m, 1024), 1024)
    proj3 = proj.reshape(bsz, t, N_PROJ)
    c_rec, tb_rec = (16, 256) if prompt else (SAMPLE_T_PAD, SAMPLE_T_PAD)
    st_layer = 0 if prompt else layer
    ya, sa_new = hgrn2(proj3, W['lb'], W['a_onorm'], sa, st_layer, c_rec, tb_rec, t_real)
    yb, sb_new, sbc_new = gdn(proj3, W['b_conv_w'], sbc, W['alog_row'], W['dt_row'], W['b_onorm'], sb, st_layer,
                              c_rec, tb_rec, t_real, 2 if prompt else GDN_SAMPLE_SEQS)
    qkv = rope_qkv(proj, cos_t, sin_t, _pick_tile(m, 1024))
    if prompt:
        os_, ls = [], []
        for gi in range(C_GROUPS):
            o, lse = dil_prompt(qkv, t, gi)
            os_.append(o)
            ls.append(lse)
    else:
        os_, ls = dil_sample(qkv, bsz, caches, layer, t_real)
    rows = []
    qkv4 = qkv.reshape(QKV_PLANES, bsz, t, DH)
    nh = C_GROUPS * HEADS
    for gi in range(C_GROUPS):
        keep = min(C_WINDOWS[gi], t_real)
        for part in (1, 2):
            p0 = part * nh + gi * HEADS
            rows.append(jnp.transpose(qkv4[p0:p0 + HEADS, :, t_real - keep:t_real], (1, 2, 0, 3)))
    tm = _pick_tile(m, 256)
    x = merge(x, ya.reshape(m, HW), yb.reshape(m, HW), os_, ls, proj,
              W['w_pa'], W['w_pb'], W['w_pc'], W['w_o'], W['g_mix_post'], tm)
    q = rms_matmul(x, W['g_x_pre'], W['w_xq'], _pick_tile(m, 1024), HW)
    o = xattn(q.reshape(bsz, t, HW), mk5, mv5, mem_layer, _pick_tile(t, 256))
    x = proj_post(x, o.reshape(m, HW), W['w_xo'], W['g_x_post'], tm)
    if prompt:
        assert t_real == t
        xo, up = conv_ffn(x.reshape(bsz, t, D_MODEL), W['g_ffn_pre'], W['w_up'], W['ffn_conv_w'], W['ffn_conv_b'],
                          W['w_down'], W['g_ffn_post'], sf, None, _pick_tile(t, 512))
        sf_new = up[:, 8 - (FFN_CONV - 1):]
    else:
        kw = FFN_CONV - 1
        nxt = jnp.concatenate([sf[1:], sf[:1]], axis=0)
        ovr = jnp.concatenate([jnp.zeros((bsz, t - kw, 2 * D_FF), F32), nxt], axis=1).reshape(1, m, 2 * D_FF)
        xo, up = conv_ffn(x.reshape(1, m, D_MODEL), W['g_ffn_pre'], W['w_up'], W['ffn_conv_w'], W['ffn_conv_b'],
                          W['w_down'], W['g_ffn_post'], sf[:1], ovr, tm)
        sf_new = up.reshape(bsz, t, 2 * D_FF)[:, t_real - kw:t_real]
    return xo.reshape(m, D_MODEL), sa_new, sb_new, sbc_new, rows, sf_new


def _rope_tables(pos):
    half = DH // 2
    inv = ROPE_THETA ** (-jnp.arange(half, dtype=F32) / half)
    ang = pos.astype(F32)[:, None] * inv[None, :]
    cos, sin = jnp.cos(ang), jnp.sin(ang)
    return jnp.concatenate([cos, cos], axis=-1), jnp.concatenate([-sin, sin], axis=-1)


def _reorder_w_in(w):
    o = 4 * HW
    a = w[:, :o]
    bqkv = w[:, o:o + B_CONV_CH]
    o += B_CONV_CH
    bz = w[:, o:o + HW]
    o += HW
    bba = w[:, o:o + 2 * HEADS]
    o += 2 * HEADS
    c = w[:, o:o + 3 * C_GROUPS * HW]
    o += 3 * C_GROUPS * HW
    gates = w[:, o:]
    pad = jnp.zeros((w.shape[0], HW - 2 * HEADS), w.dtype)
    out = jnp.concatenate([bqkv, bz, a, c, bba, pad, gates], axis=1).astype(BF16)
    assert out.shape[1] == N_PROJ
    return out


def kernel(x_prompt, x_sample, mem_prompt, state_a, state_b, state_b_conv, cache_c0_k, cache_c0_v, cache_c1_k,
           cache_c1_v, cache_c2_k, cache_c2_v, state_ffn_conv, cache_mem_k, cache_mem_v, g_mix_pre, g_mix_post,
           g_x_pre, g_x_post, g_mem, g_ffn_pre, g_ffn_post, w_in, a_lb, a_onorm, b_conv_w, b_a_log, b_dt_bias,
           b_onorm, w_pa, w_pb, w_pc, w_o, w_xq, w_xk, w_xv, w_xo, w_up, ffn_conv_w, ffn_conv_b, w_down):
    depth = w_in.shape[0]
    bp, tp, _ = x_prompt.shape
    bs, ts, _ = x_sample.shape
    tsp = SAMPLE_T_PAD
    sm = jax.nn.softmax(a_lb.astype(F32), axis=0)
    lb_all = jnp.cumsum(sm, axis=0) - sm[0]
    cos_p, sin_p = _rope_tables(jnp.arange(tp, dtype=jnp.int32))
    cos_p, sin_p = jnp.tile(cos_p, (bp, 1)), jnp.tile(sin_p, (bp, 1))
    cos_s, sin_s = _rope_tables(PAST_LEN + jnp.arange(tsp, dtype=jnp.int32))
    cos_s, sin_s = jnp.tile(cos_s, (bs, 1)), jnp.tile(sin_s, (bs, 1))
    caches = (cache_c0_k, cache_c0_v, cache_c1_k, cache_c1_v, cache_c2_k, cache_c2_v)

    xp = x_prompt.reshape(bp * tp, D_MODEL)
    xs = jnp.pad(x_sample, ((0, 0), (0, tsp - ts), (0, 0))).reshape(bs * tsp, D_MODEL)
    zeros_state = jnp.zeros((1, bp, HEADS, DH, DH), F32)
    zeros_bconv = jnp.zeros((1, bp, B_CONV - 1, B_CONV_CH), F32)
    zeros_fconv = jnp.zeros((bp, FFN_CONV - 1, 2 * D_FF), F32)
    lane8 = jnp.zeros((1, 128), F32)

    po = [[] for _ in range(12)]
    so = [[] for _ in range(10)]
    for l in range(depth):
        W = dict(g_mix_pre=g_mix_pre[l], g_mix_post=g_mix_post[l], g_x_pre=g_x_pre[l], g_x_post=g_x_post[l],
                 g_ffn_pre=g_ffn_pre[l], g_ffn_post=g_ffn_post[l], w_in=_reorder_w_in(w_in[l]), lb=lb_all[l],
                 a_onorm=a_onorm[l], b_conv_w=b_conv_w[l],
                 alog_row=lane8.at[0, HEADS:2 * HEADS].set(b_a_log[l]),
                 dt_row=lane8.at[0, HEADS:2 * HEADS].set(b_dt_bias[l]),
                 b_onorm=b_onorm[l], w_pa=w_pa[l].astype(BF16), w_pb=w_pb[l].astype(BF16),
                 w_pc=w_pc[l].astype(BF16), w_o=w_o[l].astype(BF16), w_xq=w_xq[l].astype(BF16),
                 w_xo=w_xo[l].astype(BF16), w_up=w_up[l].astype(BF16), ffn_conv_w=ffn_conv_w[l],
                 ffn_conv_b=ffn_conv_b[l], w_down=w_down[l].astype(BF16))
        w_kv = jnp.concatenate([w_xk[l], w_xv[l]], axis=1).astype(BF16)
        mkv = rms_matmul(mem_prompt.reshape(bp * N_MEM, D_MODEL), g_mem[l], w_kv, bp * N_MEM, HW)
        mkv5 = mkv.reshape(1, bp, N_MEM, 2 * HEADS, DH)
        mk = mkv5[0, :, :, :HEADS]
        mv = mkv5[0, :, :, HEADS:]
        xp, sa, sb, sbc, rows, sf = _mixer_and_ffn(
            xp, bp, tp, tp, W, zeros_state, zeros_state, zeros_bconv, None, l, zeros_fconv,
            mk[None], mv[None], 0, cos_p, sin_p)
        for i, a in enumerate([sa, sb, sbc] + rows + [sf, mk, mv]):
            po[i].append(a)
        xs, sa, sb, sbc, rows, sf = _mixer_and_ffn(
            xs, bs, tsp, ts, W, state_a, state_b, state_b_conv, caches, l, state_ffn_conv[l],
            cache_mem_k, cache_mem_v, l, cos_s, sin_s)
        for i, a in enumerate([sa, sb, sbc] + rows + [sf]):
            so[i].append(a)
    p_out = [jnp.stack(a, axis=0) for a in po]
    s_out = [jnp.stack(a, axis=0) for a in so]
    y_prompt = xp.reshape(bp, tp, D_MODEL)
    y_sample = xs.reshape(bs, tsp, D_MODEL)[:, :ts]
    return tuple([y_prompt, y_sample] + p_out + s_out)
```

```python
import functools
import math

import jax
import jax.numpy as jnp
from jax import lax
from jax.experimental import pallas as pl
from jax.experimental.pallas import tpu as pltpu

F32 = jnp.float32
BF16 = jnp.bfloat16

D_MODEL = 1024
PAST_LEN = 2048
EPS = 1e-6
NEG = -1e30
LB_FLOOR = 1e-30
ROPE_THETA = 10000.0
HEADS = 4
DH = 128
HW = HEADS * DH
B_CONV = 4
B_CONV_CH = 3 * HW
C_WINDOWS = (128, 512, 2048)
C_DILATIONS = (1, 4, 16)
C_GROUPS = 3
C_BAND = 128
C_BLK = 128
N_MEM = 256
D_FF = 128 * ((8 * D_MODEL // 3 + 127) // 128)
FFN_CONV = 3
FFN_COLS = 256
SAMPLE_T_PAD = 8
REC_SAMPLE_SEQS = 4
XATTN_SAMPLE_SEQS = 4

COL_BQKV = 0
COL_BZ = 1536
COL_A = 2048
COL_CQ = 4096
COL_CK = 5632
COL_CV = 7168
COL_BBA = 8704
COL_GATES = 9216
N_PROJ = 12288

VMEM_LIMIT_BYTES = 56 * 1024 * 1024


def _cparams(*sem):
    return pltpu.CompilerParams(dimension_semantics=sem, vmem_limit_bytes=VMEM_LIMIT_BYTES)


def _dot(a, b):
    return jnp.dot(a.astype(BF16), b.astype(BF16), preferred_element_type=F32)


def _dot_nt(a, b):
    return lax.dot_general(a.astype(BF16), b.astype(BF16), (((1,), (1,)), ((), ())), preferred_element_type=F32)


def _dot_tn(a, b):
    return lax.dot_general(a.astype(BF16), b.astype(BF16), (((0,), (0,)), ((), ())), preferred_element_type=F32)


def _cumsum_rows(tri, x):
    return jnp.dot(tri, x, precision=lax.Precision.HIGHEST, preferred_element_type=F32)


def _rms_rows(x, g):
    return x * lax.rsqrt(jnp.mean(x * x, axis=-1, keepdims=True) + EPS) * g


def _sigmoid(x):
    return jax.nn.sigmoid(x)


def _silu(x):
    return x * jax.nn.sigmoid(x)


def _log1p_exp_neg_abs(x):
    return jnp.log1p(jnp.exp(-jnp.abs(x)))


def _tri(c):
    r = lax.broadcasted_iota(jnp.int32, (c, c), 0)
    s = lax.broadcasted_iota(jnp.int32, (c, c), 1)
    return r, s


def _rms_matmul_kernel(x_ref, g_ref, w_ref, o_ref, xn_ref):
    @pl.when(pl.program_id(1) == 0)
    def _():
        xn_ref[...] = _rms_rows(x_ref[...], g_ref[...]).astype(BF16)

    o_ref[...] = jnp.dot(xn_ref[...], w_ref[...], preferred_element_type=F32)


def rms_matmul(x, g, w, tm, tn):
    m, k = x.shape
    n = w.shape[1]
    return pl.pallas_call(
        _rms_matmul_kernel,
        grid=(m // tm, n // tn),
        in_specs=[pl.BlockSpec((tm, k), lambda i, j: (i, 0)),
                  pl.BlockSpec((1, k), lambda i, j: (0, 0)),
                  pl.BlockSpec((k, tn), lambda i, j: (0, j))],
        out_specs=pl.BlockSpec((tm, tn), lambda i, j: (i, j)),
        out_shape=jax.ShapeDtypeStruct((m, n), F32),
        scratch_shapes=[pltpu.VMEM((tm, k), BF16)],
        compiler_params=_cparams("parallel", "arbitrary"),
        name="rms_matmul",
    )(x, g.reshape(1, k), w)


def _hgrn2_kernel(pq_ref, pf_ref, pi_ref, pg_ref, lb_ref, on_ref, s0_ref, y_ref, s_ref, *, c, tb, t_real, nb):
    j = pl.program_id(1)
    n_j = pl.num_programs(1)

    @pl.when(j == 0)
    def _():
        for n in range(nb):
            for h in range(HEADS):
                s_ref[n, h] = s0_ref[0, n, h].T

    lb = lb_ref[...]
    log_lb = jnp.log(jnp.maximum(lb, LB_FLOOR))
    log_1mlb = jnp.log1p(-lb)
    onorm = on_ref[...]
    r_i, s_i = _tri(c)
    tri = (r_i >= s_i).astype(F32)
    row = lax.broadcasted_iota(jnp.int32, (1, c, 1), 1)
    n_diag = min(c, t_real)

    def chunk(i, carry):
        r0 = pl.multiple_of(i * c, c)
        rows = pl.ds(r0, c)
        zq = pq_ref[:, rows, :]
        zf = pf_ref[:, rows, :]
        v = pi_ref[:, rows, :]
        zg = pg_ref[:, rows, :]
        q = _silu(zq) * (DH ** -0.5)
        log_sig = jnp.minimum(zf, 0.0) - _log1p_exp_neg_abs(zf)
        t1 = log_1mlb + log_sig
        logf = jnp.maximum(log_lb, t1) + _log1p_exp_neg_abs(log_lb - t1)
        k = -jnp.tanh(0.5 * logf) * (jnp.exp(logf) + 1.0)
        if t_real < tb:
            live = (j * tb + r0 + row) < t_real
            logf = jnp.where(live, logf, 0.0)
            k = jnp.where(live, k, 0.0)
        b = jnp.stack([_cumsum_rows(tri, logf[n]) for n in range(nb)], axis=0)
        bl = b[:, c - 1:c, :]
        qs = q * jnp.exp(b)
        kd = k * jnp.exp(bl - b)
        el = jnp.exp(bl)
        for h in range(HEADS):
            hs = slice(h * DH, (h + 1) * DH)
            o = jnp.stack([_dot_nt(qs[n, :, hs], s_ref[n, h]) for n in range(nb)], axis=0)
            bh, qh, kh, vh = b[:, :, hs], q[:, :, hs], k[:, :, hs], v[:, :, hs]
            for r8 in range(0, n_diag, 8):
                o_part = o[:, r8:, :]
                for s in range(r8, min(r8 + 8, n_diag)):
                    m = row[:, r8:, :] >= s
                    d = jnp.where(m, bh[:, r8:, :] - bh[:, s:s + 1, :], 0.0)
                    a = qh[:, r8:, :] * kh[:, s:s + 1, :] * jnp.exp(d)
                    col = jnp.where(m, jnp.sum(a, axis=-1, keepdims=True), 0.0)
                    o_part = o_part + col * vh[:, s:s + 1, :]
                o = o_part if r8 == 0 else jnp.concatenate([o[:, :r8, :], o_part], axis=1)
            for n in range(nb):
                s_ref[n, h] = s_ref[n, h] * el[n, :, hs] + _dot_tn(vh[n], kd[n, :, hs])
            y_ref[:, rows, hs] = _rms_rows(o, onorm) * _silu(zg[:, :, hs])
        return carry

    lax.fori_loop(0, tb // c, chunk, 0)

    @pl.when(j == n_j - 1)
    def _():
        for n in range(nb):
            for h in range(HEADS):
                s_ref[n, h] = s_ref[n, h].T


def hgrn2(proj3, lb, onorm, s0_all, layer, c, tb, t_real, nb):
    bsz, t, _ = proj3.shape
    assert bsz % nb == 0
    cb = COL_A // HW
    in_specs = [pl.BlockSpec((nb, tb, HW), functools.partial(lambda b, j, k: (b, j, k), k=cb + i)) for i in range(4)]
    in_specs += [pl.BlockSpec((1, HW), lambda b, j: (0, 0)),
                 pl.BlockSpec((1, DH), lambda b, j: (0, 0)),
                 pl.BlockSpec((1, nb, HEADS, DH, DH), lambda b, j: (layer, b, 0, 0, 0))]
    return pl.pallas_call(
        functools.partial(_hgrn2_kernel, c=c, tb=tb, t_real=t_real, nb=nb),
        grid=(bsz // nb, t // tb),
        in_specs=in_specs,
        out_specs=[pl.BlockSpec((nb, tb, HW), lambda b, j: (b, j, 0)),
                   pl.BlockSpec((nb, HEADS, DH, DH), lambda b, j: (b, 0, 0, 0))],
        out_shape=[jax.ShapeDtypeStruct((bsz, t, HW), F32),
                   jax.ShapeDtypeStruct((bsz, HEADS, DH, DH), F32)],
        compiler_params=_cparams("parallel", "arbitrary"),
        name="hgrn2",
    )(proj3, proj3, proj3, proj3, lb.reshape(1, HW), onorm.reshape(1, DH), s0_all)


def _gdn_kernel(x_ref, z_ref, ba_ref, cw_ref, buf_ref, alog_ref, dt_ref, on_ref, s0_ref,
                y_ref, s_ref, nbuf_ref, win_ref, act_ref, *, c, tb, t_real, nb):
    j = pl.program_id(1)
    n_j = pl.num_programs(1)
    kw = B_CONV - 1

    @pl.when(j == 0)
    def _():
        for n in range(nb):
            win_ref[n, 8 - kw:8, :] = buf_ref[0, n]
        s_ref[...] = s0_ref[0]

    for n in range(nb):
        win_ref[n, 8:8 + tb, :] = x_ref[n]
        conv = win_ref[n, 8 - kw:8 - kw + tb, :] * cw_ref[0:1, :]
        for i in range(1, B_CONV):
            conv = conv + win_ref[n, 8 - kw + i:8 - kw + i + tb, :] * cw_ref[i:i + 1, :]
        act_ref[n] = _silu(conv)
        for h in range(HEADS):
            qc = slice(h * DH, (h + 1) * DH)
            kc = slice(HW + h * DH, HW + (h + 1) * DH)
            qa = act_ref[n, :, qc]
            ka = act_ref[n, :, kc]
            act_ref[n, :, qc] = qa * lax.rsqrt(jnp.sum(qa * qa, axis=-1, keepdims=True) + EPS) * (DH ** -0.5)
            act_ref[n, :, kc] = ka * lax.rsqrt(jnp.sum(ka * ka, axis=-1, keepdims=True) + EPS)

    @pl.when(j == n_j - 1)
    def _():
        last = t_real - (t_real - 1) // tb * tb
        for n in range(nb):
            nbuf_ref[n] = win_ref[n, 8 + last - kw:8 + last, :]

    for n in range(nb):
        win_ref[n, 8 - kw:8, :] = win_ref[n, 8 + tb - kw:8 + tb, :]

    onorm = on_ref[...]
    r_i, s_i = _tri(c)
    tri = (r_i >= s_i).astype(F32)
    strict = r_i > s_i
    incl = r_i >= s_i
    row = lax.broadcasted_iota(jnp.int32, (c, 1), 0)

    def chunk(i, carry):
        r0 = pl.multiple_of(i * c, c)
        rows = pl.ds(r0, c)
        qss, qks, kds, bls, lows, rhs = [], [], [], [], [], []
        for n in range(nb):
            blk = ba_ref[n, rows, :]
            beta_all = _sigmoid(blk)
            sp_in = blk + dt_ref[...]
            softplus = jnp.maximum(sp_in, 0.0) + _log1p_exp_neg_abs(sp_in)
            g_all = -jnp.exp(alog_ref[...]) * softplus
            if t_real < tb:
                live = (j * tb + r0 + row) < t_real
                beta_all = jnp.where(live, beta_all, 0.0)
                g_all = jnp.where(live, g_all, 0.0)
            bg = _cumsum_rows(tri, g_all)
            bg_t = bg.T
            for h in range(HEADS):
                qh = act_ref[n, rows, h * DH:(h + 1) * DH]
                kh = act_ref[n, rows, HW + h * DH:HW + (h + 1) * DH]
                vh = act_ref[n, rows, 2 * HW + h * DH:2 * HW + (h + 1) * DH]
                beta = beta_all[:, h:h + 1]
                b_col = bg[:, HEADS + h:HEADS + h + 1]
                b_row = bg_t[HEADS + h:HEADS + h + 1, :]
                diff = b_col - b_row
                dec_s = jnp.where(strict, jnp.exp(jnp.where(strict, diff, 0.0)), 0.0)
                dec_i = jnp.where(incl, jnp.exp(jnp.where(incl, diff, 0.0)), 0.0)
                st = s_ref[n, h]
                eb = jnp.exp(b_col)
                bl = b_col[c - 1:c, :]
                lows.append(beta * _dot_nt(kh, kh) * dec_s)
                rhs.append(beta * (vh - eb * _dot(kh, st)))
                qss.append(eb * _dot(qh, st))
                qks.append(_dot_nt(qh, kh) * dec_i)
                kds.append(kh * jnp.exp(bl - b_col))
                bls.append(bl)
        u = jnp.concatenate(rhs, axis=1)
        for t in range(c - 1):
            lcol = jnp.concatenate([jnp.broadcast_to(lw[:, t:t + 1], (c, DH)) for lw in lows], axis=1)
            u = u - lcol * u[t:t + 1, :]
        for n in range(nb):
            z = z_ref[n, rows, :]
            for h in range(HEADS):
                e = n * HEADS + h
                hs = slice(h * DH, (h + 1) * DH)
                uh = u[:, e * DH:(e + 1) * DH]
                o = qss[e] + _dot(qks[e], uh)
                s_ref[n, h] = s_ref[n, h] * jnp.exp(bls[e]) + _dot_tn(kds[e], uh)
                y_ref[n, rows, hs] = _rms_rows(o, onorm) * _silu(z[:, hs])
        return carry

    lax.fori_loop(0, tb // c, chunk, 0)


def gdn(proj3, conv_w, conv_buf_all, alog_row, dt_row, onorm, s0_all, layer, c, tb, t_real, nb):
    bsz, t, _ = proj3.shape
    assert t_real - (t_real - 1) // tb * tb >= B_CONV - 1 and bsz % nb == 0
    return pl.pallas_call(
        functools.partial(_gdn_kernel, c=c, tb=tb, t_real=t_real, nb=nb),
        grid=(bsz // nb, t // tb),
        in_specs=[pl.BlockSpec((nb, tb, B_CONV_CH), lambda b, j: (b, j, COL_BQKV // B_CONV_CH)),
                  pl.BlockSpec((nb, tb, HW), lambda b, j: (b, j, COL_BZ // HW)),
                  pl.BlockSpec((nb, tb, 128), lambda b, j: (b, j, COL_BBA // 128)),
                  pl.BlockSpec((B_CONV, B_CONV_CH), lambda b, j: (0, 0)),
                  pl.BlockSpec((1, nb, B_CONV - 1, B_CONV_CH), lambda b, j: (layer, b, 0, 0)),
                  pl.BlockSpec((1, 128), lambda b, j: (0, 0)),
                  pl.BlockSpec((1, 128), lambda b, j: (0, 0)),
                  pl.BlockSpec((1, DH), lambda b, j: (0, 0)),
                  pl.BlockSpec((1, nb, HEADS, DH, DH), lambda b, j: (layer, b, 0, 0, 0))],
        out_specs=[pl.BlockSpec((nb, tb, HW), lambda b, j: (b, j, 0)),
                   pl.BlockSpec((nb, HEADS, DH, DH), lambda b, j: (b, 0, 0, 0)),
                   pl.BlockSpec((nb, B_CONV - 1, B_CONV_CH), lambda b, j: (b, 0, 0))],
        out_shape=[jax.ShapeDtypeStruct((bsz, t, HW), F32),
                   jax.ShapeDtypeStruct((bsz, HEADS, DH, DH), F32),
                   jax.ShapeDtypeStruct((bsz, B_CONV - 1, B_CONV_CH), F32)],
        scratch_shapes=[pltpu.VMEM((nb, tb + 8, B_CONV_CH), F32), pltpu.VMEM((nb, tb, B_CONV_CH), F32)],
        compiler_params=_cparams("parallel", "arbitrary"),
        name="gdn",
    )(proj3, proj3, proj3, conv_w, conv_buf_all, alog_row, dt_row, onorm.reshape(1, DH), s0_all)


QKV_PLANES = 3 * C_GROUPS * HEADS


def _rope_kernel(x_ref, cos_ref, sin_ref, o_ref):
    j = pl.program_id(1)

    @pl.when(j < 2 * C_GROUPS)
    def _():
        cs = cos_ref[...]
        sn = sin_ref[...]
        for h in range(HEADS):
            x = x_ref[:, h * DH:(h + 1) * DH]
            o_ref[h] = x * cs + pltpu.roll(x, DH // 2, 1) * sn

    @pl.when(j >= 2 * C_GROUPS)
    def _():
        for h in range(HEADS):
            o_ref[h] = x_ref[:, h * DH:(h + 1) * DH]


def rope_qkv(proj, cos_t, sin_t, tm):
    m = proj.shape[0]
    cb = COL_CQ // HW
    return pl.pallas_call(
        _rope_kernel,
        grid=(m // tm, 3 * C_GROUPS),
        in_specs=[pl.BlockSpec((tm, HW), lambda i, j: (i, cb + j)),
                  pl.BlockSpec((tm, DH), lambda i, j: (i, 0)),
                  pl.BlockSpec((tm, DH), lambda i, j: (i, 0))],
        out_specs=pl.BlockSpec((HEADS, tm, DH), lambda i, j: (j, i, 0)),
        out_shape=jax.ShapeDtypeStruct((QKV_PLANES, m, DH), F32),
        compiler_params=_cparams("parallel", "arbitrary"),
        name="rope_qkv",
    )(proj, cos_t, sin_t)


DIL_ROWS = C_BLK * max(C_DILATIONS)
DIL_GROUP = 4


def _dil_prompt_kernel(q_ref, kc_ref, vc_ref, kp_ref, vp_ref, o_ref, lse_ref, *, d, t_blocks):
    first = (pl.program_id(0) % t_blocks) == 0
    span = C_BLK * d
    qi, ki = _tri(C_BLK)
    in_band = ki >= qi
    mask_cur = ki <= qi
    mask_first = in_band & jnp.logical_not(first)
    items = [(blk, r) for blk in range(DIL_ROWS // span) for r in range(d)]
    for g0 in range(0, len(items), DIL_GROUP):
        group = items[g0:g0 + DIL_GROUP]
        rows = [pl.ds(blk * span + r, C_BLK, stride=d) for blk, r in group]
        qs = [q_ref[0, rw, :] for rw in rows]
        prev = []
        for blk, r in group:
            if blk > 0:
                pr = pl.ds((blk - 1) * span + r, C_BLK, stride=d)
                prev.append((kc_ref[0, pr, :], vc_ref[0, pr, :], in_band))
            else:
                pr = pl.ds(r, C_BLK, stride=d)
                prev.append((kp_ref[0, pr, :], vp_ref[0, pr, :], mask_first))
        sps = [jnp.where(pv[2], _dot_nt(q, pv[0]) * (DH ** -0.5), NEG) for q, pv in zip(qs, prev)]
        scs = [jnp.where(mask_cur, _dot_nt(q, kc_ref[0, rw, :]) * (DH ** -0.5), NEG) for q, rw in zip(qs, rows)]
        ms = [jnp.maximum(jnp.max(sp, axis=-1, keepdims=True), jnp.max(sc, axis=-1, keepdims=True))
              for sp, sc in zip(sps, scs)]
        pps = [jnp.exp(sp - m) for sp, m in zip(sps, ms)]
        pcs = [jnp.exp(sc - m) for sc, m in zip(scs, ms)]
        ls = [jnp.sum(pp, axis=-1, keepdims=True) + jnp.sum(pc, axis=-1, keepdims=True) for pp, pc in zip(pps, pcs)]
        for rw, pp, pc, pv, l, m in zip(rows, pps, pcs, prev, ls, ms):
            o_ref[rw, :] = (_dot(pp, pv[1]) + _dot(pc, vc_ref[0, rw, :])) / l
            lse_ref[rw, :] = jnp.broadcast_to(m + jnp.log(l), (C_BLK, DH))


def dil_prompt(qkv, t, gi):
    m = qkv.shape[1]
    d = C_DILATIONS[gi]
    span = C_BLK * d
    r_blk = DIL_ROWS
    assert t % r_blk == 0 and r_blk % span == 0
    per = r_blk // span
    nh = C_GROUPS * HEADS

    def cur(part):
        return pl.BlockSpec((1, r_blk, DH), lambda i, h: (part * nh + gi * HEADS + h, i, 0))

    def prev(part):
        return pl.BlockSpec((1, span, DH), lambda i, h: (part * nh + gi * HEADS + h, jnp.maximum(i * per - 1, 0), 0))

    return pl.pallas_call(
        functools.partial(_dil_prompt_kernel, d=d, t_blocks=t // r_blk),
        grid=(m // r_blk, HEADS),
        in_specs=[cur(0), cur(1), cur(2), prev(1), prev(2)],
        out_specs=[pl.BlockSpec((r_blk, DH), lambda i, h: (i, h))] * 2,
        out_shape=[jax.ShapeDtypeStruct((m, HW), F32)] * 2,
        compiler_params=_cparams("parallel", "arbitrary"),
        name="dil_prompt_g%d" % gi,
    )(qkv, qkv, qkv, qkv, qkv)


SAMPLE_QKV_ROWS = SAMPLE_T_PAD * HEADS


def _store_rows_th(o_ref, l_ref, o, lse, t, rows0):
    for h in range(HEADS):
        hs = slice(h * DH, (h + 1) * DH)
        o_ref[0, t:t + 1, hs] = o[rows0 + h:rows0 + h + 1, :]
        l_ref[0, t:t + 1, hs] = jnp.broadcast_to(lse[rows0 + h:rows0 + h + 1, :], (1, DH))


def _dil_sample_kernel(qkv_ref, k0c, v0c, k1c, v1c, k2c, v2c,
                       o0_ref, o1_ref, o2_ref, l0_ref, l1_ref, l2_ref, *, t_real):
    kc = (k0c, k1c, k2c)
    vc = (v0c, v1c, v2c)
    o_refs = (o0_ref, o1_ref, o2_ref)
    l_refs = (l0_ref, l1_ref, l2_ref)
    scale = DH ** -0.5
    nq = t_real * HEADS
    for gi in range(C_GROUPS):
        o_refs[gi][...] = jnp.zeros_like(o_refs[gi])
        l_refs[gi][...] = jnp.zeros_like(l_refs[gi])
    for gi in range(1, C_GROUPS):
        qm = qkv_ref[0, gi, 0:nq, :]
        kn = qkv_ref[0, C_GROUPS + gi, 0:nq, :]
        vn = qkv_ref[0, 2 * C_GROUPS + gi, 0:nq, :]
        kb = kc[gi][0, 0]
        vb = vc[gi][0, 0]
        s = jnp.sum(kb * qm[None], axis=-1, keepdims=True) * scale
        sn = jnp.sum(kn * qm, axis=-1, keepdims=True) * scale
        m = jnp.maximum(jnp.max(s, axis=0), sn)
        p = jnp.exp(s - m[None])
        pn = jnp.exp(sn - m)
        l = jnp.sum(p, axis=0) + pn
        o = (jnp.sum(p * vb, axis=0) + pn * vn) / l
        lse = m + jnp.log(l)
        for t in range(t_real):
            _store_rows_th(o_refs[gi], l_refs[gi], o, lse, t, t * HEADS)
    half = lax.broadcasted_iota(jnp.int32, (2 * HEADS, 1), 0) >= HEADS
    n_pair = C_BAND // 2
    pos = 2 * lax.broadcasted_iota(jnp.int32, (n_pair, 2 * HEADS, 1), 0) + half[None].astype(jnp.int32)
    t_new = 2 * lax.broadcasted_iota(jnp.int32, (nq // (2 * HEADS), 2 * HEADS, 1), 0) + half[None].astype(jnp.int32)
    qm = qkv_ref[0, 0, 0:nq, :]
    kn = qkv_ref[0, C_GROUPS, 0:nq, :].reshape(nq // (2 * HEADS), 2 * HEADS, DH)
    vn = qkv_ref[0, 2 * C_GROUPS, 0:nq, :].reshape(nq // (2 * HEADS), 2 * HEADS, DH)
    k0 = kc[0][0, 0]
    v0 = vc[0][0, 0]
    for t in range(t_real):
        pair = qm[2 * HEADS * (t // 2):2 * HEADS * (t // 2 + 1), :]
        swapped = pltpu.roll(pair, HEADS, 0)
        q8 = jnp.where(half == (t % 2 == 1), pair, swapped)
        s = jnp.sum(k0 * q8[None], axis=-1, keepdims=True) * scale
        s = jnp.where(pos >= t, s, NEG)
        sn = jnp.sum(kn * q8[None], axis=-1, keepdims=True) * scale
        sn = jnp.where(t_new <= t, sn, NEG)
        m8 = jnp.maximum(jnp.max(s, axis=0), jnp.max(sn, axis=0))
        m = jnp.maximum(m8, pltpu.roll(m8, HEADS, 0))
        p = jnp.exp(s - m[None])
        pn = jnp.exp(sn - m[None])
        l8 = jnp.sum(p, axis=0) + jnp.sum(pn, axis=0)
        l = l8 + pltpu.roll(l8, HEADS, 0)
        o8 = jnp.sum(p * v0, axis=0) + jnp.sum(pn * vn, axis=0)
        o = (o8 + pltpu.roll(o8, HEADS, 0)) / l
        _store_rows_th(o_refs[0], l_refs[0], o, m + jnp.log(l), t, 0)


def dil_sample(qkv, bsz, caches, layer, t_real):
    tp = SAMPLE_T_PAD
    assert t_real % 2 == 0
    qkv_th = qkv.reshape(3 * C_GROUPS, HEADS, bsz, tp, DH).transpose(2, 0, 3, 1, 4)
    qkv_th = qkv_th.reshape(bsz, 3 * C_GROUPS, SAMPLE_QKV_ROWS, DH)
    views = []
    specs = []
    for gi in range(C_GROUPS):
        d = C_DILATIONS[gi]
        for cch in (caches[2 * gi], caches[2 * gi + 1]):
            depth, db, ln = cch.shape[:3]
            assert ln == C_WINDOWS[gi] and ln // d == C_BAND and (gi == 0 or t_real <= d)
            if gi == 0:
                views.append(cch.reshape(depth, db, C_BAND // 2, 2 * HEADS, DH))
                specs.append(pl.BlockSpec((1, 1, C_BAND // 2, 2 * HEADS, DH),
                                          functools.partial(lambda b, l: (l, b, 0, 0, 0), l=layer)))
            else:
                views.append(cch.reshape(depth, db, C_BAND, d * HEADS, DH))
                specs.append(pl.BlockSpec((1, 1, C_BAND, t_real * HEADS, DH),
                                          functools.partial(lambda b, l: (l, b, 0, 0, 0), l=layer)))
    outs = pl.pallas_call(
        functools.partial(_dil_sample_kernel, t_real=t_real),
        grid=(bsz,),
        in_specs=[pl.BlockSpec((1, 3 * C_GROUPS, SAMPLE_QKV_ROWS, DH), lambda b: (b, 0, 0, 0))] + specs,
        out_specs=[pl.BlockSpec((1, tp, HW), lambda b: (b, 0, 0))] * (2 * C_GROUPS),
        out_shape=[jax.ShapeDtypeStruct((bsz, tp, HW), F32)] * (2 * C_GROUPS),
        compiler_params=_cparams("parallel"),
        name="dil_sample",
    )(qkv_th, *views)
    o = [a.reshape(bsz * tp, HW) for a in outs[:C_GROUPS]]
    lse = [a.reshape(bsz * tp, HW) for a in outs[C_GROUPS:]]
    return o, lse


def _merge_kernel(x_ref, ya_ref, yb_ref, o0_ref, o1_ref, o2_ref, l0_ref, l1_ref, l2_ref,
                  g0_ref, g1_ref, g2_ref, wpa_ref, wpb_ref, wpc_ref, wo_ref, gp_ref, out_ref):
    ls = (l0_ref[...], l1_ref[...], l2_ref[...])
    m = jnp.maximum(jnp.maximum(ls[0], ls[1]), ls[2])
    es = [jnp.exp(a - m) for a in ls]
    den = es[0] + es[1] + es[2]
    yc = (es[0] / den) * o0_ref[...] + (es[1] / den) * o1_ref[...] + (es[2] / den) * o2_ref[...]
    merged = (_sigmoid(g0_ref[...]) * jnp.dot(ya_ref[...].astype(BF16), wpa_ref[...], preferred_element_type=F32)
              + _sigmoid(g1_ref[...]) * jnp.dot(yb_ref[...].astype(BF16), wpb_ref[...], preferred_element_type=F32)
              + _sigmoid(g2_ref[...]) * jnp.dot(yc.astype(BF16), wpc_ref[...], preferred_element_type=F32))
    z = jnp.dot(merged.astype(BF16), wo_ref[...], preferred_element_type=F32)
    out_ref[...] = x_ref[...] + _rms_rows(z, gp_ref[...])


def merge(x, ya, yb, os_, ls, proj, wpa, wpb, wpc, wo, g_post, tm):
    m = x.shape[0]
    gb = COL_GATES // D_MODEL
    row = lambda w: pl.BlockSpec((tm, w), lambda i: (i, 0))
    const = lambda a: pl.BlockSpec(a.shape, lambda i: (0, 0), pipeline_mode=pl.Buffered(1))
    in_specs = [row(D_MODEL)] + [row(HW)] * 8
    in_specs += [pl.BlockSpec((tm, D_MODEL), functools.partial(lambda i, k: (i, k), k=gb + n)) for n in range(3)]
    gp = g_post.reshape(1, D_MODEL)
    in_specs += [const(wpa), const(wpb), const(wpc), const(wo), const(gp)]
    return pl.pallas_call(
        _merge_kernel,
        grid=(m // tm,),
        in_specs=in_specs,
        out_specs=row(D_MODEL),
        out_shape=jax.ShapeDtypeStruct((m, D_MODEL), F32),
        compiler_params=_cparams("parallel"),
        name="merge",
    )(x, ya, yb, *os_, *ls, proj, proj, proj, wpa, wpb, wpc, wo, gp)


def _xattn_prompt_kernel(q_ref, mkv_ref, o_ref):
    heads = [slice(h * DH, (h + 1) * DH) for h in range(HEADS)]
    scs = [_dot_nt(q_ref[0, :, hs], mkv_ref[0, :, hs]) * (DH ** -0.5) for hs in heads]
    ms = [jnp.max(sc, axis=-1, keepdims=True) for sc in scs]
    ps = [jnp.exp(sc - m) for sc, m in zip(scs, ms)]
    ps = [p / jnp.sum(p, axis=-1, keepdims=True) for p in ps]
    for h, (hs, p) in enumerate(zip(heads, ps)):
        o_ref[0, :, hs] = _dot(p, mkv_ref[0, :, HW + h * DH:HW + (h + 1) * DH])


def xattn_prompt(q3, mkv3, tm):
    bsz, t, _ = q3.shape
    return pl.pallas_call(
        _xattn_prompt_kernel,
        grid=(bsz, t // tm),
        in_specs=[pl.BlockSpec((1, tm, HW), lambda b, j: (b, j, 0)),
                  pl.BlockSpec((1, N_MEM, 2 * HW), lambda b, j: (b, 0, 0))],
        out_specs=pl.BlockSpec((1, tm, HW), lambda b, j: (b, j, 0)),
        out_shape=jax.ShapeDtypeStruct((bsz, t, HW), F32),
        compiler_params=_cparams("parallel", "arbitrary"),
        name="xattn_prompt",
    )(q3, mkv3)


def _xattn_sample_kernel(q_ref, mk_ref, mv_ref, o_ref, *, nb):
    rows = SAMPLE_QKV_ROWS
    r_head = lax.broadcasted_iota(jnp.int32, (rows, N_MEM * HEADS), 0) % HEADS
    c_head = lax.broadcasted_iota(jnp.int32, (rows, N_MEM * HEADS), 1) % HEADS
    own = r_head == c_head
    scs = [jnp.where(own, _dot_nt(q_ref[n], mk_ref[0, n]) * (DH ** -0.5), NEG) for n in range(nb)]
    ms = [jnp.max(sc, axis=-1, keepdims=True) for sc in scs]
    ps = [jnp.exp(sc - m) for sc, m in zip(scs, ms)]
    ps = [p / jnp.sum(p, axis=-1, keepdims=True) for p in ps]
    for n, p in enumerate(ps):
        o_ref[n] = _dot(p, mv_ref[0, n])


def xattn_sample(q_th, mk_all, mv_all, layer, nb):
    bsz = q_th.shape[0]
    depth = mk_all.shape[0]
    assert bsz % nb == 0
    mk = mk_all.reshape(depth, bsz, N_MEM * HEADS, DH)
    mv = mv_all.reshape(depth, bsz, N_MEM * HEADS, DH)
    return pl.pallas_call(
        functools.partial(_xattn_sample_kernel, nb=nb),
        grid=(bsz // nb,),
        in_specs=[pl.BlockSpec((nb, SAMPLE_QKV_ROWS, DH), lambda b: (b, 0, 0)),
                  pl.BlockSpec((1, nb, N_MEM * HEADS, DH), lambda b: (layer, b, 0, 0)),
                  pl.BlockSpec((1, nb, N_MEM * HEADS, DH), lambda b: (layer, b, 0, 0))],
        out_specs=pl.BlockSpec((nb, SAMPLE_QKV_ROWS, DH), lambda b: (b, 0, 0)),
        out_shape=jax.ShapeDtypeStruct((bsz, SAMPLE_QKV_ROWS, DH), F32),
        compiler_params=_cparams("parallel"),
        name="xattn_sample",
    )(q_th, mk, mv)


def _proj_post_kernel(x_ref, o_ref, w_ref, g_ref, out_ref):
    z = jnp.dot(o_ref[...].astype(BF16), w_ref[...], preferred_element_type=F32)
    out_ref[...] = x_ref[...] + _rms_rows(z, g_ref[...])


def proj_post(x, o, w, g, tm):
    m = x.shape[0]
    k = o.shape[1]
    gp = g.reshape(1, D_MODEL)
    return pl.pallas_call(
        _proj_post_kernel,
        grid=(m // tm,),
        in_specs=[pl.BlockSpec((tm, D_MODEL), lambda i: (i, 0)),
                  pl.BlockSpec((tm, k), lambda i: (i, 0)),
                  pl.BlockSpec(w.shape, lambda i: (0, 0), pipeline_mode=pl.Buffered(1)),
                  pl.BlockSpec((1, D_MODEL), lambda i: (0, 0), pipeline_mode=pl.Buffered(1))],
        out_specs=pl.BlockSpec((tm, D_MODEL), lambda i: (i, 0)),
        out_shape=jax.ShapeDtypeStruct((m, D_MODEL), F32),
        compiler_params=_cparams("parallel"),
        name="proj_post",
    )(x, o, w, gp)


def _ffn_kernel(*refs, tm, use_ovr):
    if use_ovr:
        (x_ref, gpre_ref, wup_ref, cw_ref, cb_ref, wdn_ref, gpost_ref, buf_ref, ovr_ref,
         out_ref, up_ref, carry_ref, win_ref) = refs
    else:
        (x_ref, gpre_ref, wup_ref, cw_ref, cb_ref, wdn_ref, gpost_ref, buf_ref,
         out_ref, up_ref, carry_ref, win_ref) = refs
    kw = FFN_CONV - 1

    @pl.when(pl.program_id(1) == 0)
    def _():
        carry_ref[8 - kw:8, :] = buf_ref[0]

    x = x_ref[0]
    h = _rms_rows(x, gpre_ref[...]).astype(BF16)
    if use_ovr:
        slot_row = lax.broadcasted_iota(jnp.int32, (tm, 1), 0) % SAMPLE_T_PAD
        is_ovr = slot_row >= SAMPLE_T_PAD - kw
    acc = jnp.zeros((tm, D_MODEL), F32)
    for cidx in range(D_FF // FFN_COLS):
        ys = []
        for half in range(2):
            c0 = half * D_FF + cidx * FFN_COLS
            cols = slice(c0, c0 + FFN_COLS)
            u = jnp.dot(h, wup_ref[:, cols], preferred_element_type=F32)
            if use_ovr:
                u = jnp.where(is_ovr, ovr_ref[0, :, cols], u)
            up_ref[0, :, cols] = u if use_ovr else u[tm - 8:tm]
            win_ref[8 - kw:8, :] = carry_ref[8 - kw:8, cols]
            win_ref[8:8 + tm, :] = u
            y = u * cw_ref[kw:kw + 1, cols] + cb_ref[:, cols]
            for i in range(kw):
                y = y + win_ref[8 - kw + i:8 - kw + i + tm, :] * cw_ref[i:i + 1, cols]
            carry_ref[8 - kw:8, cols] = win_ref[8 + tm - kw:8 + tm, :]
            ys.append(y)
        y1, y2 = ys
        gelu = 0.5 * y1 * (1.0 + jnp.tanh(math.sqrt(2.0 / math.pi) * (y1 + 0.044715 * (y1 * y1 * y1))))
        a = (gelu * y2).astype(BF16)
        acc = acc + jnp.dot(a, wdn_ref[cidx * FFN_COLS:(cidx + 1) * FFN_COLS, :], preferred_element_type=F32)
    out_ref[0] = x + _rms_rows(acc, gpost_ref[...])


def conv_ffn(x3, g_pre, w_up, cw, cb, w_down, g_post, buf0, ovr3, tm):
    bsz, t, _ = x3.shape
    use_ovr = ovr3 is not None
    const = lambda a: pl.BlockSpec(a.shape, lambda b, j: (0,) * a.ndim, pipeline_mode=pl.Buffered(1))
    gpre = g_pre.reshape(1, D_MODEL)
    gpost = g_post.reshape(1, D_MODEL)
    cb2 = cb.reshape(1, 2 * D_FF)
    in_specs = [pl.BlockSpec((1, tm, D_MODEL), lambda b, j: (b, j, 0)),
                const(gpre), const(w_up), const(cw), const(cb2), const(w_down), const(gpost),
                pl.BlockSpec((1, FFN_CONV - 1, 2 * D_FF), lambda b, j: (b, 0, 0))]
    args = [x3, gpre, w_up, cw, cb2, w_down, gpost, buf0]
    if use_ovr:
        in_specs.append(pl.BlockSpec((1, tm, 2 * D_FF), lambda b, j: (b, j, 0)))
        args.append(ovr3)
    return pl.pallas_call(
        functools.partial(_ffn_kernel, tm=tm, use_ovr=use_ovr),
        grid=(bsz, t // tm),
        in_specs=in_specs,
        out_specs=[pl.BlockSpec((1, tm, D_MODEL), lambda b, j: (b, j, 0)),
                   pl.BlockSpec((1, tm, 2 * D_FF), lambda b, j: (b, j, 0)) if use_ovr
                   else pl.BlockSpec((1, 8, 2 * D_FF), lambda b, j: (b, 0, 0))],
        out_shape=[jax.ShapeDtypeStruct((bsz, t, D_MODEL), F32),
                   jax.ShapeDtypeStruct((bsz, t if use_ovr else 8, 2 * D_FF), F32)],
        scratch_shapes=[pltpu.VMEM((8, 2 * D_FF), F32), pltpu.VMEM((tm + 8, FFN_COLS), F32)],
        compiler_params=_cparams("parallel", "arbitrary"),
        name="conv_ffn",
    )(*args)


def _pick_tile(m, pref):
    t = min(m, pref)
    assert m % t == 0
    return t


def _mixer_and_ffn(x, bsz, t, t_real, W, sa, sb, sbc, caches, layer, sf, mem_k, mem_v, cos_t, sin_t):
    m = bsz * t
    prompt = caches is None
    proj = rms_matmul(x, W['g_mix_pre'], W['w_in'], _pick_tile(m, 1024), 1024)
    proj3 = proj.reshape(bsz, t, N_PROJ)
    c_rec, tb_rec = (16, 256) if prompt else (SAMPLE_T_PAD, SAMPLE_T_PAD)
    st_layer = 0 if prompt else layer
    n_seq = 2 if prompt else REC_SAMPLE_SEQS
    ya, sa_new = hgrn2(proj3, W['lb'], W['a_onorm'], sa, st_layer, c_rec, tb_rec, t_real, n_seq)
    yb, sb_new, sbc_new = gdn(proj3, W['b_conv_w'], sbc, W['alog_row'], W['dt_row'], W['b_onorm'], sb, st_layer,
                              c_rec, tb_rec, t_real, n_seq)
    qkv = rope_qkv(proj, cos_t, sin_t, _pick_tile(m, 1024))
    if prompt:
        os_, ls = [], []
        for gi in range(C_GROUPS):
            o, lse = dil_prompt(qkv, t, gi)
            os_.append(o)
            ls.append(lse)
    else:
        os_, ls = dil_sample(qkv, bsz, caches, layer, t_real)
    rows = []
    qkv4 = qkv.reshape(QKV_PLANES, bsz, t, DH)
    nh = C_GROUPS * HEADS
    for gi in range(C_GROUPS):
        keep = min(C_WINDOWS[gi], t_real)
        for part in (1, 2):
            p0 = part * nh + gi * HEADS
            rows.append(jnp.transpose(qkv4[p0:p0 + HEADS, :, t_real - keep:t_real], (1, 2, 0, 3)))
    tm = _pick_tile(m, 256)
    x = merge(x, ya.reshape(m, HW), yb.reshape(m, HW), os_, ls, proj,
              W['w_pa'], W['w_pb'], W['w_pc'], W['w_o'], W['g_mix_post'], tm)
    q = rms_matmul(x, W['g_x_pre'], W['w_xq'], _pick_tile(m, 1024), HW)
    if prompt:
        o = xattn_prompt(q.reshape(bsz, t, HW), mem_k, _pick_tile(t, 256))
    else:
        o = xattn_sample(q.reshape(bsz, SAMPLE_QKV_ROWS, DH), mem_k, mem_v, layer, XATTN_SAMPLE_SEQS)
    x = proj_post(x, o.reshape(m, HW), W['w_xo'], W['g_x_post'], tm)
    if prompt:
        assert t_real == t
        xo, up = conv_ffn(x.reshape(bsz, t, D_MODEL), W['g_ffn_pre'], W['w_up'], W['ffn_conv_w'], W['ffn_conv_b'],
                          W['w_down'], W['g_ffn_post'], sf, None, _pick_tile(t, 512))
        sf_new = up[:, 8 - (FFN_CONV - 1):]
    else:
        kw = FFN_CONV - 1
        nxt = jnp.concatenate([sf[1:], sf[:1]], axis=0)
        ovr = jnp.concatenate([jnp.zeros((bsz, t - kw, 2 * D_FF), F32), nxt], axis=1).reshape(1, m, 2 * D_FF)
        xo, up = conv_ffn(x.reshape(1, m, D_MODEL), W['g_ffn_pre'], W['w_up'], W['ffn_conv_w'], W['ffn_conv_b'],
                          W['w_down'], W['g_ffn_post'], sf[:1], ovr, tm)
        sf_new = up.reshape(bsz, t, 2 * D_FF)[:, t_real - kw:t_real]
    return xo.reshape(m, D_MODEL), sa_new, sb_new, sbc_new, rows, sf_new


def _rope_tables(pos):
    half = DH // 2
    inv = ROPE_THETA ** (-jnp.arange(half, dtype=F32) / half)
    ang = pos.astype(F32)[:, None] * inv[None, :]
    cos, sin = jnp.cos(ang), jnp.sin(ang)
    return jnp.concatenate([cos, cos], axis=-1), jnp.concatenate([-sin, sin], axis=-1)


def _reorder_w_in(w):
    o = 4 * HW
    a = w[:, :o]
    bqkv = w[:, o:o + B_CONV_CH]
    o += B_CONV_CH
    bz = w[:, o:o + HW]
    o += HW
    bba = w[:, o:o + 2 * HEADS]
    o += 2 * HEADS
    c = w[:, o:o + 3 * C_GROUPS * HW]
    o += 3 * C_GROUPS * HW
    gates = w[:, o:]
    pad = jnp.zeros((w.shape[0], HW - 2 * HEADS), w.dtype)
    out = jnp.concatenate([bqkv, bz, a, c, bba, pad, gates], axis=1).astype(BF16)
    assert out.shape[1] == N_PROJ
    return out


def kernel(x_prompt, x_sample, mem_prompt, state_a, state_b, state_b_conv, cache_c0_k, cache_c0_v, cache_c1_k,
           cache_c1_v, cache_c2_k, cache_c2_v, state_ffn_conv, cache_mem_k, cache_mem_v, g_mix_pre, g_mix_post,
           g_x_pre, g_x_post, g_mem, g_ffn_pre, g_ffn_post, w_in, a_lb, a_onorm, b_conv_w, b_a_log, b_dt_bias,
           b_onorm, w_pa, w_pb, w_pc, w_o, w_xq, w_xk, w_xv, w_xo, w_up, ffn_conv_w, ffn_conv_b, w_down):
    depth = w_in.shape[0]
    bp, tp, _ = x_prompt.shape
    bs, ts, _ = x_sample.shape
    tsp = SAMPLE_T_PAD
    sm = jax.nn.softmax(a_lb.astype(F32), axis=0)
    lb_all = jnp.cumsum(sm, axis=0) - sm[0]
    cos_p, sin_p = _rope_tables(jnp.arange(tp, dtype=jnp.int32))
    cos_p, sin_p = jnp.tile(cos_p, (bp, 1)), jnp.tile(sin_p, (bp, 1))
    cos_s, sin_s = _rope_tables(PAST_LEN + jnp.arange(tsp, dtype=jnp.int32))
    cos_s, sin_s = jnp.tile(cos_s, (bs, 1)), jnp.tile(sin_s, (bs, 1))
    caches = (cache_c0_k, cache_c0_v, cache_c1_k, cache_c1_v, cache_c2_k, cache_c2_v)

    xp = x_prompt.reshape(bp * tp, D_MODEL)
    xs = jnp.pad(x_sample, ((0, 0), (0, tsp - ts), (0, 0))).reshape(bs * tsp, D_MODEL)
    zeros_state = jnp.zeros((1, bp, HEADS, DH, DH), F32)
    zeros_bconv = jnp.zeros((1, bp, B_CONV - 1, B_CONV_CH), F32)
    zeros_fconv = jnp.zeros((bp, FFN_CONV - 1, 2 * D_FF), F32)
    lane8 = jnp.zeros((1, 128), F32)

    po = [[] for _ in range(12)]
    so = [[] for _ in range(10)]
    for l in range(depth):
        W = dict(g_mix_pre=g_mix_pre[l], g_mix_post=g_mix_post[l], g_x_pre=g_x_pre[l], g_x_post=g_x_post[l],
                 g_ffn_pre=g_ffn_pre[l], g_ffn_post=g_ffn_post[l], w_in=_reorder_w_in(w_in[l]), lb=lb_all[l],
                 a_onorm=a_onorm[l], b_conv_w=b_conv_w[l],
                 alog_row=lane8.at[0, HEADS:2 * HEADS].set(b_a_log[l]),
                 dt_row=lane8.at[0, HEADS:2 * HEADS].set(b_dt_bias[l]),
                 b_onorm=b_onorm[l], w_pa=w_pa[l].astype(BF16), w_pb=w_pb[l].astype(BF16),
                 w_pc=w_pc[l].astype(BF16), w_o=w_o[l].astype(BF16), w_xq=w_xq[l].astype(BF16),
                 w_xo=w_xo[l].astype(BF16), w_up=w_up[l].astype(BF16), ffn_conv_w=ffn_conv_w[l],
                 ffn_conv_b=ffn_conv_b[l], w_down=w_down[l].astype(BF16))
        w_kv = jnp.concatenate([w_xk[l], w_xv[l]], axis=1).astype(BF16)
        mkv = rms_matmul(mem_prompt.reshape(bp * N_MEM, D_MODEL), g_mem[l], w_kv, bp * N_MEM, HW)
        mkv3 = mkv.reshape(bp, N_MEM, 2 * HW)
        mk = mkv3[:, :, :HW].reshape(bp, N_MEM, HEADS, DH)
        mv = mkv3[:, :, HW:].reshape(bp, N_MEM, HEADS, DH)
        xp, sa, sb, sbc, rows, sf = _mixer_and_ffn(
            xp, bp, tp, tp, W, zeros_state, zeros_state, zeros_bconv, None, l, zeros_fconv,
            mkv3, None, cos_p, sin_p)
        for i, a in enumerate([sa, sb, sbc] + rows + [sf, mk, mv]):
            po[i].append(a)
        xs, sa, sb, sbc, rows, sf = _mixer_and_ffn(
            xs, bs, tsp, ts, W, state_a, state_b, state_b_conv, caches, l, state_ffn_conv[l],
            cache_mem_k, cache_mem_v, cos_s, sin_s)
        for i, a in enumerate([sa, sb, sbc] + rows + [sf]):
            so[i].append(a)
    p_out = [jnp.stack(a, axis=0) for a in po]
    s_out = [jnp.stack(a, axis=0) for a in so]
    y_prompt = xp.reshape(bp, tp, D_MODEL)
    y_sample = xs.reshape(bs, tsp, D_MODEL)[:, :ts]
    return tuple([y_prompt, y_sample] + p_out + s_out)
```
